```python
import math
import jax, jax.numpy as jnp
from jax import lax
import numpy as np

D_MODEL = 1024
BATCH = 16
SEQ = 2048
DEPTH = 1

CTX_LEN = 256
GRID_W = 64
N_DIR = 2
EPS = 1e-6
N_MOD = 6
ML_HEADS = 4
ML_DQK = 128
ML_DV = 256
ML_CHUNK = 64
DN_HEADS = 8
DN_DK = 128
DN_DV = 128
DN_CHUNK = 64
DN_CONV = 5
PEER_HEADS = 8
PEER_NKEYS = 128
PEER_NEXP = PEER_NKEYS * PEER_NKEYS
PEER_DQ = 256
PEER_TOPK = 16
PEER_TOK_BLOCK = 128

ML_QK_W = ML_HEADS * ML_DQK
ML_V_W = ML_HEADS * ML_DV
DN_QK_W = DN_HEADS * DN_DK
DN_V_W = DN_HEADS * DN_DV
PROJ_SIZES = (ML_QK_W, ML_QK_W, ML_V_W, ML_V_W, N_DIR * ML_HEADS, N_DIR * ML_HEADS,
              DN_QK_W, DN_QK_W, DN_V_W, DN_V_W, N_DIR * DN_HEADS, N_DIR * DN_HEADS,
              D_MODEL, D_MODEL)
PROJ_W = sum(PROJ_SIZES)

kernel_name = 'hybrid_mlstm_gdn_peer_diffusion_block'


def _rmsnorm(x, w):
    xf = x.astype(jnp.float32)
    y = xf * lax.rsqrt(jnp.mean(xf * xf, axis=-1, keepdims=True) + EPS)
    return (y * w.astype(jnp.float32)).astype(x.dtype)


def _rms_f32(t):
    t = t.astype(jnp.float32)
    return t * lax.rsqrt(jnp.mean(t * t, axis=-1, keepdims=True) + EPS)


def _l2norm(t):
    return t * lax.rsqrt(jnp.sum(t * t, axis=-1, keepdims=True) + EPS)


def _split(p):
    return jnp.split(p, np.cumsum(PROJ_SIZES)[:-1].tolist(), axis=-1)


def _heads(t, n):
    B, L, _ = t.shape
    return t.reshape(B, L, n, -1).transpose(0, 2, 1, 3)


def _merge_heads(t):
    B, n, L, d = t.shape
    return t.transpose(0, 2, 1, 3).reshape(B, L, n * d)


def _dir_heads(t, n):
    B, L, _ = t.shape
    return t.reshape(B, L, N_DIR, n).transpose(2, 0, 3, 1)


def _to_colmajor(t, rows):
    B, _, C = t.shape
    return t.reshape(B, rows, GRID_W, C).transpose(0, 2, 1, 3).reshape(B, GRID_W * rows, C)


def _from_colmajor(t, rows):
    B, _, C = t.shape
    return t.reshape(B, GRID_W, rows, C).transpose(0, 2, 1, 3).reshape(B, rows * GRID_W, C)


def _chunk(t, size):
    B, H, L = t.shape[:3]
    return jnp.moveaxis(t.reshape(B, H, L // size, size, *t.shape[3:]), 2, 0)


def _unchunk(t):
    n, B, H, T, d = t.shape
    return jnp.moveaxis(t, 0, 2).reshape(B, H, n * T, d)


def _short_conv(t, w):
    K, C = w.shape
    return lax.conv_general_dilated(t, w[:, None, :].astype(t.dtype), window_strides=(1,),
                                    padding=[(K // 2, K // 2)],
                                    dimension_numbers=('NWC', 'WIO', 'NWC'),
                                    feature_group_count=C)


def _modulation(cond, w_mod, b_mod):
    m = jax.nn.silu(cond) @ w_mod + b_mod
    return [t[:, None, :] for t in jnp.split(m, N_MOD, axis=-1)]


def _mlstm_scan(q, k, v, ig, lf, state):
    T = ML_CHUNK
    q, k, v, ig, lf = (_chunk(t, T) for t in (q, k, v, ig, lf))
    b = jnp.cumsum(lf, axis=-1)
    causal = jnp.tril(jnp.ones((T, T), dtype=bool))
    dlog = jnp.where(causal, b[..., :, None] - b[..., None, :] + ig[..., None, :], -jnp.inf)
    m_intra = jnp.max(dlog, axis=-1)
    w_intra = jnp.exp(dlog - m_intra[..., None]) * jnp.einsum('nbhtk,nbhsk->nbhts', q, k)
    num_intra = jnp.einsum('nbhts,nbhsv->nbhtv', w_intra, v)
    den_intra = jnp.sum(w_intra, axis=-1)
    b_last = b[..., -1]
    g_s = b_last[..., None] - b + ig
    m_chunk = jnp.max(g_s, axis=-1)
    e_s = jnp.exp(g_s - m_chunk[..., None])
    kv_chunk = jnp.einsum('nbhs,nbhsk,nbhsv->nbhkv', e_s, k, v)
    k_chunk = jnp.einsum('nbhs,nbhsk->nbhk', e_s, k)

    def step(carry, inp):
        C, n, m = carry
        q_c, b_c, m_in, num_in, den_in, bl, m_ch, kv_ch, k_ch = inp
        m_inter = b_c + m[..., None]
        m_t = jnp.maximum(m_inter, m_in)
        a = jnp.exp(m_inter - m_t)
        r = jnp.exp(m_in - m_t)
        num = a[..., None] * jnp.einsum('bhtk,bhkv->bhtv', q_c, C) + r[..., None] * num_in
        den = a * jnp.einsum('bhtk,bhk->bht', q_c, n) + r * den_in
        h = num / jnp.maximum(jnp.abs(den), jnp.exp(-m_t))[..., None]
        m_new = jnp.maximum(bl + m, m_ch)
        sp = jnp.exp(bl + m - m_new)
        sc = jnp.exp(m_ch - m_new)
        C = sp[..., None, None] * C + sc[..., None, None] * kv_ch
        n = sp[..., None] * n + sc[..., None] * k_ch
        return (C, n, m_new), h

    state, h = lax.scan(step, state, (q, b, m_intra, num_intra, den_intra, b_last, m_chunk, kv_chunk, k_chunk))
    return _unchunk(h), state


def _gdn_scan(q, k, v, beta, g, S):
    T = DN_CHUNK
    q, k, v, beta, g = (_chunk(t, T) for t in (q, k, v, beta, g))
    gc = jnp.cumsum(g, axis=-1)
    incl = jnp.tril(jnp.ones((T, T), dtype=bool))
    strict = jnp.tril(jnp.ones((T, T), dtype=bool), k=-1)
    gam = jnp.exp(jnp.where(incl, gc[..., :, None] - gc[..., None, :], -jnp.inf))
    kb = k * beta[..., None]
    m_low = jnp.where(strict, jnp.einsum('nbhtk,nbhsk->nbhts', kb, k) * gam, 0.0)
    eye = jnp.eye(T, dtype=m_low.dtype)
    t_inv = lax.linalg.triangular_solve(eye + m_low, jnp.broadcast_to(eye, m_low.shape),
                                        left_side=True, lower=True, unit_diagonal=True)
    u = t_inv @ (v * beta[..., None])
    w = t_inv @ (kb * jnp.exp(gc)[..., None])
    a_qk = jnp.einsum('nbhtk,nbhsk->nbhts', q, k) * gam
    qg = q * jnp.exp(gc)[..., None]
    g_last = gc[..., -1]
    kd = k * jnp.exp(g_last[..., None] - gc)[..., None]
    d_last = jnp.exp(g_last)

    def step(S, inp):
        u_c, w_c, qg_c, a_c, kd_c, dl = inp
        v_new = u_c - w_c @ S
        o = qg_c @ S + a_c @ v_new
        S = dl[..., None, None] * S + jnp.swapaxes(kd_c, -1, -2) @ v_new
        return S, o

    S, o = lax.scan(step, S, (u, w, qg, a_qk, kd, d_last))
    return _unchunk(o), S


def _flip_time(t):
    return jnp.flip(t, axis=2)


def _bidirectional(scan_fn, lat_in, ctx_in, init):
    lat_shared, lat_gates = lat_in
    ctx_shared, ctx_gates = ctx_in
    outs_lat, outs_ctx = [], []
    for d in range(N_DIR):
        if d == 0:
            cs, cg = ctx_shared, [gt[d] for gt in ctx_gates]
            ls, lg = lat_shared, [gt[d] for gt in lat_gates]
        else:
            cs, cg = [_flip_time(t) for t in ctx_shared], [_flip_time(gt[d]) for gt in ctx_gates]
            ls, lg = [_flip_time(t) for t in lat_shared], [_flip_time(gt[d]) for gt in lat_gates]
        h_c, state = scan_fn(*cs, *cg, init)
        h_l, _ = scan_fn(*ls, *lg, state)
        if d == 1:
            h_c, h_l = _flip_time(h_c), _flip_time(h_l)
        outs_lat.append(h_l)
        outs_ctx.append(h_c)
    return outs_lat[0] + outs_lat[1], outs_ctx[0] + outs_ctx[1]


def _mlstm_inputs(q, k, v, i_raw, f_raw, gate_bias):
    f32 = jnp.float32
    gb = gate_bias.astype(f32)
    q = _heads(q, ML_HEADS).astype(f32) * ML_DQK ** -0.5
    k = _heads(k, ML_HEADS).astype(f32)
    v = _heads(v, ML_HEADS).astype(f32)
    ig = _dir_heads(i_raw, ML_HEADS).astype(f32) + gb[0][:, None, :, None]
    lf = jax.nn.log_sigmoid(_dir_heads(f_raw, ML_HEADS).astype(f32) + gb[1][:, None, :, None])
    return (q, k, v), (ig, lf)


def _gdn_inputs(qkv, b_raw, a_raw, a_log, dt_bias):
    f32 = jnp.float32
    qkv = jax.nn.silu(qkv.astype(f32))
    q, k, v = jnp.split(qkv, 3, axis=-1)
    q = _l2norm(_heads(q, DN_HEADS)) * DN_DK ** -0.5
    k = _l2norm(_heads(k, DN_HEADS))
    v = _heads(v, DN_HEADS)
    beta = jax.nn.sigmoid(_dir_heads(b_raw, DN_HEADS).astype(f32))
    g = -jnp.exp(a_log.astype(f32))[:, None, :, None] * jax.nn.softplus(
        _dir_heads(a_raw, DN_HEADS).astype(f32) + dt_bias.astype(f32)[:, None, :, None])
    return (q, k, v), (beta, g)


def _merge_branches(hm, hd, o_raw, z_raw, ga, gb, ml_norm_w, dn_norm_w, w_a, w_b, w_out):
    dt = o_raw.dtype
    B, L, _ = o_raw.shape
    ym = _rms_f32(hm.reshape(B, L, ML_HEADS, ML_DV)) * ml_norm_w.astype(jnp.float32)
    ym = ym.reshape(B, L, ML_V_W) * jax.nn.sigmoid(o_raw.astype(jnp.float32))
    yd = _rms_f32(hd.reshape(B, L, DN_HEADS, DN_DV)) * dn_norm_w.astype(jnp.float32)
    yd = yd.reshape(B, L, DN_V_W) * jax.nn.silu(z_raw.astype(jnp.float32))
    y = jax.nn.sigmoid(ga) * (ym.astype(dt) @ w_a) + jax.nn.sigmoid(gb) * (yd.astype(dt) @ w_b)
    return y @ w_out


def _token_mixer(h_lat, h_ctx, want_ctx, w_in, ml_gate_bias, dn_a_log, dn_dt_bias, dn_conv_w,
                 ml_norm_w, dn_norm_w, w_branch_a, w_branch_b, w_out):
    B, L, _ = h_lat.shape
    rows = L // GRID_W
    pl = _split(h_lat @ w_in)
    pc = _split(h_ctx @ w_in)
    ml_zero = (jnp.zeros((B, ML_HEADS, ML_DQK, ML_DV), jnp.float32),
               jnp.zeros((B, ML_HEADS, ML_DQK), jnp.float32),
               jnp.zeros((B, ML_HEADS), jnp.float32))
    hm_lat, hm_ctx = _bidirectional(
        _mlstm_scan,
        _mlstm_inputs(pl[0], pl[1], pl[2], pl[4], pl[5], ml_gate_bias),
        _mlstm_inputs(pc[0], pc[1], pc[2], pc[4], pc[5], ml_gate_bias),
        ml_zero)
    qkv_l = jnp.concatenate(pl[6:9], axis=-1)
    c3 = qkv_l.shape[-1]
    qkv_l = _short_conv(_to_colmajor(qkv_l, rows).reshape(B * GRID_W, rows, c3), dn_conv_w).reshape(B, L, c3)
    qkv_c = _short_conv(jnp.concatenate(pc[6:9], axis=-1), dn_conv_w)
    dn_lat = _gdn_inputs(qkv_l, _to_colmajor(pl[10], rows), _to_colmajor(pl[11], rows), dn_a_log, dn_dt_bias)
    dn_ctx = _gdn_inputs(qkv_c, pc[10], pc[11], dn_a_log, dn_dt_bias)
    hd_lat, hd_ctx = _bidirectional(_gdn_scan, dn_lat, dn_ctx,
                                    jnp.zeros((B, DN_HEADS, DN_DK, DN_DV), jnp.float32))
    hd_lat = _from_colmajor(_merge_heads(hd_lat), rows)
    y_lat = _merge_branches(_merge_heads(hm_lat), hd_lat, pl[3], pl[9], pl[12], pl[13],
                            ml_norm_w, dn_norm_w, w_branch_a, w_branch_b, w_out)
    y_ctx = None
    if want_ctx:
        y_ctx = _merge_branches(_merge_heads(hm_ctx), _merge_heads(hd_ctx), pc[3], pc[9], pc[12], pc[13],
                                ml_norm_w, dn_norm_w, w_branch_a, w_branch_b, w_out)
    return y_lat, y_ctx


def _peer(h, w_q, sub_keys, u_tab, v_tab):
    B, L, D = h.shape
    f32 = jnp.float32
    q = (h @ w_q).astype(f32).reshape(B, L, PEER_HEADS, 2, PEER_DQ // 2)
    s = jnp.einsum('blhpd,hpnd->blhpn', q, sub_keys.astype(f32))
    s_top, i_top = lax.top_k(s, PEER_TOPK)
    cand = (s_top[..., 0, :, None] + s_top[..., 1, None, :]).reshape(B, L, PEER_HEADS, PEER_TOPK * PEER_TOPK)
    cand_idx = (i_top[..., 0, :, None] * PEER_NKEYS + i_top[..., 1, None, :]).reshape(B, L, PEER_HEADS, PEER_TOPK * PEER_TOPK)
    best, pos = lax.top_k(cand, PEER_TOPK)
    idx = jnp.take_along_axis(cand_idx, pos, axis=-1)
    gate = jax.nn.softmax(best, axis=-1)
    n_blk = (B * L) // PEER_TOK_BLOCK
    e_per_tok = PEER_HEADS * PEER_TOPK
    hb = h.reshape(n_blk, PEER_TOK_BLOCK, D)
    ib = idx.reshape(n_blk, PEER_TOK_BLOCK, e_per_tok)
    gb = gate.reshape(n_blk, PEER_TOK_BLOCK, e_per_tok)

    def block(args):
        xb, i_b, g_b = args
        u = jnp.take(u_tab, i_b, axis=0)
        act = jax.nn.gelu(jnp.einsum('ted,td->te', u, xb).astype(f32), approximate=False)
        vv = jnp.take(v_tab, i_b, axis=0)
        return jnp.einsum('te,ted->td', (g_b * act).astype(xb.dtype), vv)

    y = lax.map(block, (hb, ib, gb))
    return y.reshape(B, L, D)


def setup_inputs(seed: int = 0) -> dict:
    key = jax.random.key(seed)
    ks = jax.random.split(key, 24)
    nrm = jax.random.normal
    D = D_MODEL
    x = nrm(ks[0], (BATCH, SEQ, D), jnp.float32)
    c = nrm(ks[1], (BATCH, D), jnp.float32)
    ctx = nrm(ks[2], (BATCH, CTX_LEN, D), jnp.float32)
    c_ctx = nrm(ks[3], (D,), jnp.float32)
    w_mod = nrm(ks[4], (DEPTH, D, N_MOD * D), jnp.float32) * (0.5 * D ** -0.5)
    b_mod = 0.02 * nrm(ks[5], (DEPTH, N_MOD * D), jnp.float32)
    norm_w = 1.0 + 0.05 * nrm(ks[6], (DEPTH, 4, D), jnp.float32)
    w_in = nrm(ks[7], (DEPTH, D, PROJ_W), jnp.float32) * D ** -0.5
    i_bias = -1.0 + 0.1 * nrm(ks[8], (DEPTH, 1, N_DIR, ML_HEADS), jnp.float32)
    f_bias = 3.0 + 0.5 * nrm(ks[9], (DEPTH, 1, N_DIR, ML_HEADS), jnp.float32)
    ml_gate_bias = jnp.concatenate([i_bias, f_bias], axis=1)
    dn_a_log = jnp.log(jax.random.uniform(ks[10], (DEPTH, N_DIR, DN_HEADS), jnp.float32, 1.0, 16.0))
    dt = jnp.exp(jax.random.uniform(ks[11], (DEPTH, N_DIR, DN_HEADS), jnp.float32,
                                    math.log(1e-3), math.log(1e-1)))
    dn_dt_bias = dt + jnp.log(-jnp.expm1(-dt))
    dn_conv_w = nrm(ks[12], (DEPTH, DN_CONV, 3 * DN_V_W), jnp.float32) * DN_CONV ** -0.5
    ml_norm_w = 1.0 + 0.05 * nrm(ks[13], (DEPTH, ML_HEADS, ML_DV), jnp.float32)
    dn_norm_w = 1.0 + 0.05 * nrm(ks[14], (DEPTH, DN_DV), jnp.float32)
    w_branch_a = nrm(ks[15], (DEPTH, ML_V_W, D), jnp.float32) * ML_V_W ** -0.5
    w_branch_b = nrm(ks[16], (DEPTH, DN_V_W, D), jnp.float32) * DN_V_W ** -0.5
    w_out = nrm(ks[17], (DEPTH, D, D), jnp.float32) * D ** -0.5
    peer_wq = nrm(ks[18], (DEPTH, D, PEER_HEADS * PEER_DQ), jnp.float32) * D ** -0.5
    peer_keys = nrm(ks[19], (DEPTH, PEER_HEADS, 2, PEER_NKEYS, PEER_DQ // 2), jnp.float32) * (PEER_DQ // 2) ** -0.5
    peer_u = nrm(ks[20], (DEPTH, PEER_NEXP, D), jnp.float32) * D ** -0.5
    peer_v = nrm(ks[21], (DEPTH, PEER_NEXP, D), jnp.float32) * D ** -0.5
    return {'x': x, 'c': c, 'ctx': ctx, 'c_ctx': c_ctx, 'w_mod': w_mod, 'b_mod': b_mod,
            'norm_w': norm_w, 'w_in': w_in, 'ml_gate_bias': ml_gate_bias, 'dn_a_log': dn_a_log,
            'dn_dt_bias': dn_dt_bias, 'dn_conv_w': dn_conv_w, 'ml_norm_w': ml_norm_w,
            'dn_norm_w': dn_norm_w, 'w_branch_a': w_branch_a, 'w_branch_b': w_branch_b,
            'w_out': w_out, 'peer_wq': peer_wq, 'peer_keys': peer_keys, 'peer_u': peer_u,
            'peer_v': peer_v}


def reference(x, c, ctx, c_ctx, w_mod, b_mod, norm_w, w_in, ml_gate_bias, dn_a_log, dn_dt_bias,
              dn_conv_w, ml_norm_w, dn_norm_w, w_branch_a, w_branch_b, w_out,
              peer_wq, peer_keys, peer_u, peer_v):
    lat, cx = x, ctx
    for l in range(DEPTH):
        last = l == DEPTH - 1
        sh1, sc1, g1, sh2, sc2, g2 = _modulation(c, w_mod[l], b_mod[l])
        sh1c, sc1c, g1c, sh2c, sc2c, g2c = _modulation(c_ctx[None, :], w_mod[l], b_mod[l])
        h_lat = _rmsnorm(lat, norm_w[l, 0]) * (1 + sc1) + sh1
        h_ctx = _rmsnorm(cx, norm_w[l, 0]) * (1 + sc1c) + sh1c
        y_lat, y_ctx = _token_mixer(h_lat, h_ctx, not last, w_in[l], ml_gate_bias[l], dn_a_log[l],
                                    dn_dt_bias[l], dn_conv_w[l], ml_norm_w[l], dn_norm_w[l],
                                    w_branch_a[l], w_branch_b[l], w_out[l])
        lat = lat + g1 * _rmsnorm(y_lat, norm_w[l, 1])
        h2 = _rmsnorm(lat, norm_w[l, 2]) * (1 + sc2) + sh2
        lat = lat + g2 * _rmsnorm(_peer(h2, peer_wq[l], peer_keys[l], peer_u[l], peer_v[l]), norm_w[l, 3])
        if not last:
            cx = cx + g1c * _rmsnorm(y_ctx, norm_w[l, 1])
            h2c = _rmsnorm(cx, norm_w[l, 2]) * (1 + sc2c) + sh2c
            cx = cx + g2c * _rmsnorm(_peer(h2c, peer_wq[l], peer_keys[l], peer_u[l], peer_v[l]), norm_w[l, 3])
    return lat
```

```python
import functools
import math

import jax
import jax.numpy as jnp
from jax import lax
from jax.experimental import pallas as pl
from jax.experimental.pallas import tpu as pltpu

f32 = jnp.float32
bf16 = jnp.bfloat16

EPS = 1e-6
GRID_W = 64
N_DIR = 2
N_MOD = 6
ML_HEADS, ML_DQK, ML_DV = 4, 128, 256
DN_HEADS, DN_DK, DN_DV, DN_CONV = 8, 128, 128, 5
CHUNK = 64
PEER_HEADS, PEER_NKEYS, PEER_DQ, PEER_TOPK = 8, 128, 256, 16
LANE = 128
VMEM_LIMIT = 56 * 1024 * 1024

NT = (((1,), (1,)), ((), ()))
TN = (((0,), (0,)), ((), ()))
NEG_INF = float("-inf")


def _params(sem):
    return pltpu.CompilerParams(dimension_semantics=sem, vmem_limit_bytes=VMEM_LIMIT)


def _mod_kernel(c_ref, w_ref, b_ref, o_ref):
    c = c_ref[...]
    s = c * jax.nn.sigmoid(c)
    o_ref[...] = jnp.dot(s, w_ref[...], preferred_element_type=f32) + b_ref[...]


def _modulation(cond, w_mod, b_mod):
    n, d = cond.shape
    return pl.pallas_call(
        _mod_kernel,
        out_shape=jax.ShapeDtypeStruct((n, N_MOD * d), f32),
        grid=(N_MOD,),
        in_specs=[pl.BlockSpec((n, d), lambda j: (0, 0)),
                  pl.BlockSpec((d, d), lambda j: (0, j)),
                  pl.BlockSpec((1, d), lambda j: (0, j))],
        out_specs=pl.BlockSpec((n, d), lambda j: (0, j)),
        compiler_params=_params(("arbitrary",)),
    )(cond, w_mod, b_mod.reshape(1, -1))


def _proj_kernel(x_ref, nw_ref, sc_ref, sh_ref, w_ref, o_ref, *, head_major):
    x = x_ref[0]
    ms = jnp.mean(x * x, axis=-1, keepdims=True)
    h = (x * lax.rsqrt(ms + EPS)) * nw_ref[...]
    h = h * (1.0 + sc_ref[0]) + sh_ref[0]
    acc = jnp.dot(h.astype(bf16), w_ref[...], preferred_element_type=f32)
    if head_major:
        for i in range(acc.shape[1] // LANE):
            o_ref[0, i] = acc[:, i * LANE:(i + 1) * LANE]
    else:
        o_ref[0] = acc


def _project(xs, nw, sc, sh, w, *, tn, tt, head_major):
    B, L, D = xs.shape
    N = w.shape[1]
    per_batch = sc.shape[0] == B and B > 1
    mod_map = (lambda j, b, t: (b, 0, 0)) if per_batch else (lambda j, b, t: (0, 0, 0))
    if head_major:
        out_shape = jax.ShapeDtypeStruct((B, N // LANE, L, LANE), f32)
        out_spec = pl.BlockSpec((1, tn // LANE, tt, LANE), lambda j, b, t: (b, j, t, 0))
    else:
        out_shape = jax.ShapeDtypeStruct((B, L, N), f32)
        out_spec = pl.BlockSpec((1, tt, tn), lambda j, b, t: (b, t, j))
    return pl.pallas_call(
        functools.partial(_proj_kernel, head_major=head_major),
        out_shape=out_shape,
        grid=(N // tn, B, L // tt),
        in_specs=[pl.BlockSpec((1, tt, D), lambda j, b, t: (b, t, 0)),
                  pl.BlockSpec((1, D), lambda j, b, t: (0, 0)),
                  pl.BlockSpec((1, 1, D), mod_map),
                  pl.BlockSpec((1, 1, D), mod_map),
                  pl.BlockSpec((D, tn), lambda j, b, t: (0, j))],
        out_specs=out_spec,
        compiler_params=_params(("arbitrary", "arbitrary", "arbitrary")),
    )(xs, nw, sc, sh, w)


def _chunk_masks(reverse):
    row = lax.broadcasted_iota(jnp.int32, (CHUNK, CHUNK), 0)
    col = lax.broadcasted_iota(jnp.int32, (CHUNK, CHUNK), 1)
    eye = row == col
    if reverse:
        return eye, col >= row, row >= col
    return eye, col <= row, row <= col


def _row_to_col(v_row, eye):
    return jnp.sum(jnp.where(eye, jnp.broadcast_to(v_row, (CHUNK, CHUNK)), 0.0), axis=1, keepdims=True)


def _cumsum_forms(v_row, eye, incl, incl_t):
    vb = jnp.broadcast_to(v_row, (CHUNK, CHUNK))
    c_col = jnp.sum(jnp.where(incl, vb, 0.0), axis=1, keepdims=True)
    v_col = jnp.sum(jnp.where(eye, vb, 0.0), axis=1, keepdims=True)
    c_row = jnp.sum(jnp.where(incl_t, v_col, 0.0), axis=0, keepdims=True)
    return c_col, c_row


def _ml_chunk(qs, k, v, ig_row, lf_row, c_ref, n_ref, d, m, masks, want_out):
    eye, incl, incl_t = masks
    b_col, b_row = _cumsum_forms(lf_row, eye, incl, incl_t)
    b_last = jnp.sum(lf_row, axis=1, keepdims=True)
    g_row = b_last - b_row + ig_row
    m_chunk = jnp.max(g_row, axis=1, keepdims=True)
    e_col = _row_to_col(jnp.exp(g_row - m_chunk), eye)
    ek = k * e_col
    kv = lax.dot_general(ek, v, TN, preferred_element_type=f32)
    kc = jnp.sum(ek, axis=0, keepdims=True)
    c_old = c_ref[d]
    n_old = n_ref[d]
    h = None
    if want_out:
        dlog = jnp.where(incl, b_col - b_row + ig_row, NEG_INF)
        m_in = jnp.max(dlog, axis=1, keepdims=True)
        qk = lax.dot_general(qs, k, NT, preferred_element_type=f32)
        w_in = jnp.exp(dlog - m_in) * qk
        num_in = jnp.dot(w_in, v, preferred_element_type=f32)
        den_in = jnp.sum(w_in, axis=1, keepdims=True)
        m_inter = b_col + m
        m_t = jnp.maximum(m_inter, m_in)
        a = jnp.exp(m_inter - m_t)
        r = jnp.exp(m_in - m_t)
        num = a * jnp.dot(qs, c_old, preferred_element_type=f32) + r * num_in
        den = a * jnp.sum(qs * n_old, axis=1, keepdims=True) + r * den_in
        h = num / jnp.maximum(jnp.abs(den), jnp.exp(-m_t))
    m_new = jnp.maximum(b_last + m, m_chunk)
    sp = jnp.exp(b_last + m - m_new)
    sc = jnp.exp(m_chunk - m_new)
    c_ref[d] = sp * c_old + sc * kv
    n_ref[d] = sp * n_old + sc * kc
    return h, m_new


def _mlstm_kernel(q_ref, k_ref, v_ref, kc_ref, vc_ref, gi_ref, gf_ref, gic_ref, gfc_ref, bias_ref,
                  o_ref, c_ref, n_ref):
    n_lat = q_ref.shape[1] // CHUNK
    n_ctx = kc_ref.shape[1] // CHUNK
    scale = ML_DQK ** -0.5
    c_ref[...] = jnp.zeros_like(c_ref)
    n_ref[...] = jnp.zeros_like(n_ref)
    o_ref[...] = jnp.zeros_like(o_ref)
    masks = [_chunk_masks(False), _chunk_masks(True)]

    def gates(i_ref, f_ref, d, c):
        ig = i_ref[0, d, 0, pl.ds(c, 1), :] + bias_ref[0, d, 0]
        lf = jax.nn.log_sigmoid(f_ref[0, d, 0, pl.ds(c, 1), :] + bias_ref[1, d, 0])
        return ig, lf

    def ctx_body(i, ms):
        out = []
        for d in range(N_DIR):
            c = i if d == 0 else n_ctx - 1 - i
            sl = pl.ds(pl.multiple_of(c * CHUNK, CHUNK), CHUNK)
            ig, lf = gates(gic_ref, gfc_ref, d, c)
            _, m_new = _ml_chunk(None, kc_ref[0, sl, :], vc_ref[0, sl, :], ig, lf, c_ref, n_ref, d,
                                 ms[d], masks[d], False)
            out.append(m_new)
        return tuple(out)

    def lat_body(i, ms):
        out = []
        for d in range(N_DIR):
            c = i if d == 0 else n_lat - 1 - i
            sl = pl.ds(pl.multiple_of(c * CHUNK, CHUNK), CHUNK)
            ig, lf = gates(gi_ref, gf_ref, d, c)
            h, m_new = _ml_chunk(q_ref[0, sl, :] * scale, k_ref[0, sl, :], v_ref[0, sl, :], ig, lf,
                                 c_ref, n_ref, d, ms[d], masks[d], True)
            o_ref[0, sl, :] += h
            out.append(m_new)
        return tuple(out)

    m0 = (jnp.zeros((1, 1), f32), jnp.zeros((1, 1), f32))
    ms = lax.fori_loop(0, n_ctx, ctx_body, m0)
    lax.fori_loop(0, n_lat, lat_body, ms)


def _mlstm(p_lat, p_ctx, gi, gf, gic, gfc, bias):
    B, L, _ = p_lat.shape
    Lc = p_ctx.shape[1]
    n_lat, n_ctx = L // CHUNK, Lc // CHUNK
    qk_blocks = ML_HEADS * ML_DQK // ML_DQK
    v_blocks = 2 * ML_HEADS * ML_DQK // ML_DV
    gate_spec = lambda n: pl.BlockSpec((1, N_DIR, 1, n, CHUNK), lambda b, h: (b, 0, h, 0, 0))
    return pl.pallas_call(
        _mlstm_kernel,
        out_shape=jax.ShapeDtypeStruct((B, L, ML_HEADS * ML_DV), f32),
        grid=(B, ML_HEADS),
        in_specs=[pl.BlockSpec((1, L, ML_DQK), lambda b, h: (b, 0, h)),
                  pl.BlockSpec((1, L, ML_DQK), lambda b, h: (b, 0, qk_blocks + h)),
                  pl.BlockSpec((1, L, ML_DV), lambda b, h: (b, 0, v_blocks + h)),
                  pl.BlockSpec((1, Lc, ML_DQK), lambda b, h: (b, 0, qk_blocks + h)),
                  pl.BlockSpec((1, Lc, ML_DV), lambda b, h: (b, 0, v_blocks + h)),
                  gate_spec(n_lat), gate_spec(n_lat), gate_spec(n_ctx), gate_spec(n_ctx),
                  pl.BlockSpec((2, N_DIR, 1, 1, CHUNK), lambda b, h: (0, 0, h, 0, 0))],
        out_specs=pl.BlockSpec((1, L, ML_DV), lambda b, h: (b, 0, h)),
        scratch_shapes=[pltpu.VMEM((N_DIR, ML_DQK, ML_DV), f32), pltpu.VMEM((N_DIR, 1, ML_DQK), f32)],
        compiler_params=_params(("arbitrary", "arbitrary")),
    )(p_lat, p_lat, p_lat, p_ctx, p_ctx, gi, gf, gic, gfc, bias)


def _dn_prep_kernel(x_ref, w_ref, o_ref, pad_ref, *, seq_len, n_cols):
    kind = pl.program_id(1) // DN_HEADS
    half = DN_CONV // 2
    width = n_cols * LANE
    pad_ref[0:8, :] = jnp.zeros((8, width), f32)
    pad_ref[8 + seq_len:16 + seq_len, :] = jnp.zeros((8, width), f32)
    pad_ref[8:8 + seq_len, :] = x_ref[0, 0]
    y = jnp.zeros((seq_len, width), f32)
    for t in range(DN_CONV):
        y = y + pad_ref[pl.ds(8 - half + t, seq_len), :] * w_ref[0, t:t + 1, :]
    y = y * jax.nn.sigmoid(y)
    q_scale = jnp.where(kind == 0, DN_DK ** -0.5, 1.0).astype(f32)
    is_v = kind == 2
    for c in range(n_cols):
        seg = y[:, c * LANE:(c + 1) * LANE]
        ss = jnp.sum(seg * seg, axis=-1, keepdims=True)
        fac = jnp.where(is_v, 1.0, lax.rsqrt(ss + EPS) * q_scale)
        seg = seg * fac
        if seq_len >= CHUNK:
            for i in range(seq_len // CHUNK):
                o_ref[0, 0, i] = seg[i * CHUNK:(i + 1) * CHUNK]
        else:
            per = CHUNK // seq_len
            o_ref[0, 0, c // per, (c % per) * seq_len:(c % per + 1) * seq_len, :] = seg


def _dn_prep(raw, wt, *, seq_len, n_cols):
    B, G, L, _ = raw.shape
    n_chunks = L // CHUNK
    view = raw.reshape(B, G, seq_len, n_cols * LANE)
    return pl.pallas_call(
        functools.partial(_dn_prep_kernel, seq_len=seq_len, n_cols=n_cols),
        out_shape=jax.ShapeDtypeStruct((B, G, n_chunks, CHUNK, LANE), f32),
        grid=(B, G),
        in_specs=[pl.BlockSpec((1, 1, seq_len, n_cols * LANE), lambda b, g: (b, g, 0, 0)),
                  pl.BlockSpec((1, 8, n_cols * LANE), lambda b, g: (g, 0, 0))],
        out_specs=pl.BlockSpec((1, 1, n_chunks, CHUNK, LANE), lambda b, g: (b, g, 0, 0, 0)),
        scratch_shapes=[pltpu.VMEM((seq_len + 16, n_cols * LANE), f32)],
        compiler_params=_params(("arbitrary", "arbitrary")),
    )(view, wt)


def _dn_chunk(q, k, v, braw_row, araw_row, a_scale, dt_bias, s_ref, d, masks, want_out):
    eye, incl, incl_t = masks
    strict = jnp.logical_and(incl, jnp.logical_not(eye))
    eye_f = eye.astype(f32)
    beta_col = _row_to_col(jax.nn.sigmoid(braw_row), eye)
    g_row = a_scale * jax.nn.softplus(araw_row + dt_bias)
    gc_col, gc_row = _cumsum_forms(g_row, eye, incl, incl_t)
    g_last = jnp.sum(g_row, axis=1, keepdims=True)
    gam = jnp.exp(jnp.where(incl, gc_col - gc_row, NEG_INF))
    kb = k * beta_col
    m_mat = jnp.where(strict, lax.dot_general(kb, k, NT, preferred_element_type=f32) * gam, 0.0)
    hi = lax.Precision.HIGHEST
    inv = eye_f - m_mat
    pw = jnp.dot(m_mat, m_mat, precision=hi, preferred_element_type=f32)
    for step in range(5):
        inv = inv + jnp.dot(inv, pw, precision=hi, preferred_element_type=f32)
        if step < 4:
            pw = jnp.dot(pw, pw, precision=hi, preferred_element_type=f32)
    egc = jnp.exp(gc_col)
    u = jnp.dot(inv, v * beta_col, preferred_element_type=f32)
    w = jnp.dot(inv, kb * egc, preferred_element_type=f32)
    kd = k * jnp.exp(g_last - gc_col)
    s_old = s_ref[d]
    v_new = u - jnp.dot(w, s_old, preferred_element_type=f32)
    o = None
    if want_out:
        a_qk = lax.dot_general(q, k, NT, preferred_element_type=f32) * gam
        o = (jnp.dot(q * egc, s_old, preferred_element_type=f32)
             + jnp.dot(a_qk, v_new, preferred_element_type=f32))
    s_ref[d] = jnp.exp(g_last) * s_old + lax.dot_general(kd, v_new, TN, preferred_element_type=f32)
    return o


def _gdn_kernel(q_ref, k_ref, v_ref, kc_ref, vc_ref, be_ref, ar_ref, bec_ref, arc_ref, alog_ref, dtb_ref,
                o_ref, s_ref, acc_ref, *, rows):
    n_lat = q_ref.shape[2]
    n_ctx = kc_ref.shape[2]
    s_ref[...] = jnp.zeros_like(s_ref)
    acc_ref[...] = jnp.zeros_like(acc_ref)
    masks = [_chunk_masks(False), _chunk_masks(True)]

    def run(i, n_chunks, qr, kr, vr, ber, arr, want_out):
        for d in range(N_DIR):
            c = i if d == 0 else n_chunks - 1 - i
            a_scale = -jnp.exp(alog_ref[d, 0])
            q = qr[0, 0, c] if want_out else None
            o = _dn_chunk(q, kr[0, 0, c], vr[0, 0, c], ber[0, d, 0, pl.ds(c, 1), :],
                          arr[0, d, 0, pl.ds(c, 1), :], a_scale, dtb_ref[d, 0], s_ref, d, masks[d], want_out)
            if want_out:
                acc_ref[c] += o

    def ctx_body(i, carry):
        run(i, n_ctx, None, kc_ref, vc_ref, bec_ref, arc_ref, False)
        return carry

    def lat_body(i, carry):
        run(i, n_lat, q_ref, k_ref, v_ref, be_ref, ar_ref, True)
        return carry

    lax.fori_loop(0, n_ctx, ctx_body, 0)
    lax.fori_loop(0, n_lat, lat_body, 0)
    per = CHUNK // rows
    for c in range(GRID_W):
        o_ref[0, 0, :, c * LANE:(c + 1) * LANE] = acc_ref[c // per, (c % per) * rows:(c % per + 1) * rows, :]


def _gdn(qkv_lat, qkv_ctx, be, ar, bec, arc, alog, dtb, *, rows):
    B, _, n_lat, _, _ = qkv_lat.shape
    n_ctx = qkv_ctx.shape[2]
    L = n_lat * CHUNK
    H = DN_HEADS
    blk = lambda n, off: pl.BlockSpec((1, 1, n, CHUNK, LANE), lambda b, h: (b, off + h, 0, 0, 0))
    gate_spec = lambda n: pl.BlockSpec((1, N_DIR, 1, n, CHUNK), lambda b, h: (b, 0, h, 0, 0))
    const_spec = pl.BlockSpec((N_DIR, 1, 1, CHUNK), lambda b, h: (0, h, 0, 0))
    out = pl.pallas_call(
        functools.partial(_gdn_kernel, rows=rows),
        out_shape=jax.ShapeDtypeStruct((B, H, rows, GRID_W * LANE), f32),
        grid=(B, H),
        in_specs=[blk(n_lat, 0), blk(n_lat, H), blk(n_lat, 2 * H), blk(n_ctx, H), blk(n_ctx, 2 * H),
                  gate_spec(n_lat), gate_spec(n_lat), gate_spec(n_ctx), gate_spec(n_ctx),
                  const_spec, const_spec],
        out_specs=pl.BlockSpec((1, 1, rows, GRID_W * LANE), lambda b, h: (b, h, 0, 0)),
        scratch_shapes=[pltpu.VMEM((N_DIR, DN_DK, DN_DV), f32), pltpu.VMEM((n_lat, CHUNK, DN_DV), f32)],
        compiler_params=_params(("arbitrary", "arbitrary")),
    )(qkv_lat, qkv_lat, qkv_lat, qkv_ctx, qkv_ctx, be, ar, bec, arc, alog, dtb)
    return out.reshape(B, H, L, LANE)


def _merge_kernel(x_ref, hm_ref, hd_ref, o_ref_in, z_ref, ga_ref, gb_ref, mlw_ref, dnw_ref, wa_ref, wb_ref,
                  wo_ref, nw1_ref, nw2_ref, g1_ref, sc2_ref, sh2_ref, lat_ref, h2_ref):
    hm = hm_ref[0]
    parts = []
    for h in range(ML_HEADS):
        seg = hm[:, h * ML_DV:(h + 1) * ML_DV]
        ms = jnp.mean(seg * seg, axis=-1, keepdims=True)
        parts.append(seg * lax.rsqrt(ms + EPS))
    ym = jnp.concatenate(parts, axis=-1) * mlw_ref[...] * jax.nn.sigmoid(o_ref_in[0])
    parts = []
    for h in range(DN_HEADS):
        seg = hd_ref[0, h]
        ms = jnp.mean(seg * seg, axis=-1, keepdims=True)
        parts.append(seg * lax.rsqrt(ms + EPS))
    z = z_ref[0]
    yd = jnp.concatenate(parts, axis=-1) * dnw_ref[...] * (z * jax.nn.sigmoid(z))
    ya = jnp.dot(ym.astype(bf16), wa_ref[...], preferred_element_type=f32)
    yb = jnp.dot(yd.astype(bf16), wb_ref[...], preferred_element_type=f32)
    y = jax.nn.sigmoid(ga_ref[0]) * ya + jax.nn.sigmoid(gb_ref[0]) * yb
    ymix = jnp.dot(y.astype(bf16), wo_ref[...], preferred_element_type=f32)
    ms = jnp.mean(ymix * ymix, axis=-1, keepdims=True)
    lat = x_ref[0] + g1_ref[0] * (ymix * lax.rsqrt(ms + EPS) * nw1_ref[...])
    lat_ref[0] = lat
    ms = jnp.mean(lat * lat, axis=-1, keepdims=True)
    h2 = (lat * lax.rsqrt(ms + EPS) * nw2_ref[...]) * (1.0 + sc2_ref[0]) + sh2_ref[0]
    h2_ref[0] = h2.astype(bf16)


def _merge(x, hm, hd, p_lat, mlw, dnw, wa, wb, wo, nw1, nw2, g1, sc2, sh2, *, tt):
    B, L, D = x.shape
    tok = lambda off: pl.BlockSpec((1, tt, D), lambda b, t: (b, t, off))
    full = lambda a: pl.BlockSpec(a.shape, lambda b, t: (0,) * a.ndim)
    per_b = pl.BlockSpec((1, 1, D), lambda b, t: (b, 0, 0))
    return pl.pallas_call(
        _merge_kernel,
        out_shape=(jax.ShapeDtypeStruct((B, L, D), f32), jax.ShapeDtypeStruct((B, L, D), bf16)),
        grid=(B, L // tt),
        in_specs=[tok(0), tok(0),
                  pl.BlockSpec((1, DN_HEADS, tt, LANE), lambda b, t: (b, 0, t, 0)),
                  tok(2), tok(3), tok(4), tok(5),
                  full(mlw), full(dnw), full(wa), full(wb), full(wo), full(nw1), full(nw2),
                  per_b, per_b, per_b],
        out_specs=(tok(0), tok(0)),
        compiler_params=_params(("arbitrary", "arbitrary")),
    )(x, hm, hd, p_lat, p_lat, p_lat, p_lat, mlw, dnw, wa, wb, wo, nw1, nw2, g1, sc2, sh2)


def _cand_pairs():
    return [(i, j) for i in range(PEER_TOPK) for j in range(PEER_TOPK) if (i + 1) * (j + 1) <= PEER_TOPK]


def _top_rows(s, count):
    rows = []
    for r in range(count):
        mx = jnp.max(s, axis=0, keepdims=True)
        rows.append(mx)
        if r + 1 < count:
            s = jnp.where(s == mx, NEG_INF, s)
    return rows


def _peer_select(h2, wq_ref, keys_ref, s2_ref, f2_ref, thr_ref, e1_ref, cand_ref):
    q = jnp.dot(h2, wq_ref[...], preferred_element_type=f32)
    pairs = _cand_pairs()
    half = PEER_DQ // 2
    for h in range(PEER_HEADS):
        s1 = lax.dot_general(keys_ref[h, 0], q[:, h * PEER_DQ:h * PEER_DQ + half], NT,
                             preferred_element_type=f32)
        s2 = lax.dot_general(keys_ref[h, 1], q[:, h * PEER_DQ + half:(h + 1) * PEER_DQ], NT,
                             preferred_element_type=f32)
        a = _top_rows(s1, PEER_TOPK)
        b = _top_rows(s2, PEER_TOPK)
        cand_ref[...] = jnp.full(cand_ref.shape, NEG_INF, f32)
        for c, (i, j) in enumerate(pairs):
            cand_ref[c:c + 1, :] = a[i] + b[j]
        cand = cand_ref[...]
        tau = _top_rows(cand, PEER_TOPK)[-1]
        m = a[0] + b[0]
        z = jnp.sum(jnp.where(cand >= tau, jnp.exp(cand - m), 0.0), axis=0, keepdims=True)
        thr = jnp.full(s1.shape, jnp.inf, f32)
        for j in range(PEER_TOPK):
            thr = jnp.where(s1 + b[j] >= tau, b[j], thr)
        s2_ref[h] = s2
        f2_ref[h] = jnp.exp(s2 - b[0]) / z
        thr_ref[h] = thr
        e1_ref[h] = jnp.exp(s1 - a[0])


def _peer_kernel(h2_ref, lat_ref, wq_ref, keys_ref, u_ref, vt_ref, g2_ref, nw_ref, o_ref,
                 s2_ref, f2_ref, thr_ref, e1_ref, cand_ref, wg_ref, yt_ref):
    j = pl.program_id(1)
    n_blk = u_ref.shape[0] // PEER_NKEYS

    @pl.when(j == 0)
    def _():
        _peer_select(h2_ref[...], wq_ref, keys_ref, s2_ref, f2_ref, thr_ref, e1_ref, cand_ref)
        yt_ref[...] = jnp.zeros_like(yt_ref)

    at = lax.dot_general(u_ref[...], h2_ref[...], NT, preferred_element_type=f32)
    for blk in range(n_blk):
        i1 = j * n_blk + blk
        a = at[blk * PEER_NKEYS:(blk + 1) * PEER_NKEYS]
        act = 0.5 * a * (1.0 + lax.erf(a * (2.0 ** -0.5)))
        w = jnp.zeros_like(a)
        for h in range(PEER_HEADS):
            th = thr_ref[h, pl.ds(i1, 1), :]
            ee = e1_ref[h, pl.ds(i1, 1), :]
            w = w + jnp.where(s2_ref[h] >= th, f2_ref[h], 0.0) * ee
        wg_ref[blk * PEER_NKEYS:(blk + 1) * PEER_NKEYS, :] = (w * act).astype(bf16)
    yt_ref[...] += jnp.dot(vt_ref[...], wg_ref[...], preferred_element_type=f32)

    @pl.when(j == pl.num_programs(1) - 1)
    def _():
        y = yt_ref[...].T
        ms = jnp.mean(y * y, axis=-1, keepdims=True)
        o_ref[...] = lat_ref[...] + g2_ref[0] * (y * lax.rsqrt(ms + EPS) * nw_ref[...])


def _peer(h2, lat, wq, keys, u, vt, g2, nw, *, tokens_per_batch, tt, ec):
    N, D = h2.shape
    E = u.shape[0]
    tiles_per_batch = tokens_per_batch // tt
    sel = lambda: pltpu.VMEM((PEER_HEADS, PEER_NKEYS, tt), f32)
    return pl.pallas_call(
        _peer_kernel,
        out_shape=jax.ShapeDtypeStruct((N, D), f32),
        grid=(N // tt, E // ec),
        in_specs=[pl.BlockSpec((tt, D), lambda i, j: (i, 0)),
                  pl.BlockSpec((tt, D), lambda i, j: (i, 0)),
                  pl.BlockSpec(wq.shape, lambda i, j: (0, 0)),
                  pl.BlockSpec(keys.shape, lambda i, j: (0, 0, 0, 0)),
                  pl.BlockSpec((ec, D), lambda i, j: (j, 0)),
                  pl.BlockSpec((D, ec), lambda i, j: (0, j)),
                  pl.BlockSpec((1, 1, D), lambda i, j: (i // tiles_per_batch, 0, 0)),
                  pl.BlockSpec((1, D), lambda i, j: (0, 0))],
        out_specs=pl.BlockSpec((tt, D), lambda i, j: (i, 0)),
        scratch_shapes=[sel(), sel(), sel(), sel(),
                        pltpu.VMEM((CHUNK, tt), f32),
                        pltpu.VMEM((ec, tt), bf16),
                        pltpu.VMEM((D, tt), f32)],
        compiler_params=_params(("arbitrary", "arbitrary")),
    )(h2, lat, wq, keys, u, vt, g2, nw)


def _dir_rows(t, n_heads):
    B, L, _ = t.shape
    return t.reshape(B, L, N_DIR, n_heads).transpose(0, 2, 3, 1).reshape(B, N_DIR, n_heads, L // CHUNK, CHUNK)


def _to_colmajor(t, rows):
    B, _, C = t.shape
    return t.reshape(B, rows, GRID_W, C).transpose(0, 2, 1, 3).reshape(B, GRID_W * rows, C)


def _pick(n, prefs):
    for p in prefs:
        if n % p == 0:
            return p
    raise ValueError(f"no tile in {prefs} divides {n}")


def kernel(x, c, ctx, c_ctx, w_mod, b_mod, norm_w, w_in, ml_gate_bias, dn_a_log, dn_dt_bias, dn_conv_w,
           ml_norm_w, dn_norm_w, w_branch_a, w_branch_b, w_out, peer_wq, peer_keys, peer_u, peer_v):
    B, L, D = x.shape
    Lc = ctx.shape[1]
    assert w_mod.shape[0] == 1, "single-layer block"
    assert L % GRID_W == 0 and L % CHUNK == 0 and Lc % CHUNK == 0
    rows = L // GRID_W
    assert rows % 8 == 0 and CHUNK % rows == 0
    assert D == ML_HEADS * ML_DV == DN_HEADS * DN_DV

    mod = _modulation(jnp.concatenate([c, c_ctx[None, :]], axis=0), w_mod[0], b_mod[0])
    mod = mod.reshape(B + 1, N_MOD, 1, D)
    sh1, sc1, g1, sh2, sc2, g2 = (mod[:B, i] for i in range(N_MOD))
    sh1c, sc1c = mod[B:, 0], mod[B:, 1]

    w = w_in[0]
    sizes = (512, 512, 1024, 1024, 8, 8, 1024, 1024, 1024, 1024, 16, 16, 1024, 1024)
    offs = [0]
    for s in sizes:
        offs.append(offs[-1] + s)
    col = lambda i: w[:, offs[i]:offs[i + 1]]
    w_main = jnp.concatenate([col(0), col(1), col(2), col(3), col(9), col(12), col(13)], axis=1).astype(bf16)
    w_dn = jnp.concatenate([col(6), col(7), col(8)], axis=1).astype(bf16)
    w_gate = jnp.concatenate([col(4), col(5), col(10), col(11)], axis=1)
    n_gate = w_gate.shape[1]
    w_gate = jnp.pad(w_gate, ((0, 0), (0, LANE - n_gate))).astype(bf16)

    nw = norm_w[0]
    tt_lat = _pick(L, (512, 256, 128, 64))
    tt_ctx = _pick(Lc, (256, 128, 64))
    proj = lambda xs, sc_, sh_, tt: (
        _project(xs, nw[0:1], sc_, sh_, w_main, tn=1024, tt=tt, head_major=False),
        _project(xs, nw[0:1], sc_, sh_, w_dn, tn=1024, tt=tt, head_major=True),
        _project(xs, nw[0:1], sc_, sh_, w_gate, tn=LANE, tt=tt, head_major=False))
    p_lat, dn_lat, g_lat = proj(x, sc1, sh1, tt_lat)
    p_ctx, dn_ctx, g_ctx = proj(ctx, sc1c, sh1c, tt_ctx)

    nh = N_DIR * ML_HEADS
    nd = N_DIR * DN_HEADS
    gi, gf = _dir_rows(g_lat[..., 0:nh], ML_HEADS), _dir_rows(g_lat[..., nh:2 * nh], ML_HEADS)
    gic, gfc = _dir_rows(g_ctx[..., 0:nh], ML_HEADS), _dir_rows(g_ctx[..., nh:2 * nh], ML_HEADS)
    be = _dir_rows(_to_colmajor(g_lat[..., 2 * nh:2 * nh + nd], rows), DN_HEADS)
    ar = _dir_rows(_to_colmajor(g_lat[..., 2 * nh + nd:2 * nh + 2 * nd], rows), DN_HEADS)
    bec = _dir_rows(g_ctx[..., 2 * nh:2 * nh + nd], DN_HEADS)
    arc = _dir_rows(g_ctx[..., 2 * nh + nd:2 * nh + 2 * nd], DN_HEADS)
    bias = jnp.broadcast_to(ml_gate_bias[0].astype(f32)[:, :, :, None, None], (2, N_DIR, ML_HEADS, 1, CHUNK))
    alog = jnp.broadcast_to(dn_a_log[0].astype(f32)[:, :, None, None], (N_DIR, DN_HEADS, 1, CHUNK))
    dtb = jnp.broadcast_to(dn_dt_bias[0].astype(f32)[:, :, None, None], (N_DIR, DN_HEADS, 1, CHUNK))

    hm = _mlstm(p_lat, p_ctx, gi, gf, gic, gfc, bias)

    taps = dn_conv_w[0].astype(f32).reshape(DN_CONV, 3 * DN_HEADS, LANE).transpose(1, 0, 2)
    taps = jnp.pad(taps, ((0, 0), (0, 8 - DN_CONV), (0, 0)))
    qkv_lat = _dn_prep(dn_lat, jnp.tile(taps, (1, 1, GRID_W)), seq_len=rows, n_cols=GRID_W)
    qkv_ctx = _dn_prep(dn_ctx, taps, seq_len=Lc, n_cols=1)
    hd = _gdn(qkv_lat, qkv_ctx, be, ar, bec, arc, alog, dtb, rows=rows)

    lat1, h2 = _merge(x, hm, hd, p_lat, ml_norm_w[0].reshape(1, D).astype(f32),
                      jnp.tile(dn_norm_w[0].astype(f32), DN_HEADS).reshape(1, D),
                      w_branch_a[0].astype(bf16), w_branch_b[0].astype(bf16), w_out[0].astype(bf16),
                      nw[1:2], nw[2:3], g1, sc2, sh2, tt=_pick(L, (256, 128, 64)))

    out = _peer(h2.reshape(B * L, D), lat1.reshape(B * L, D), peer_wq[0].astype(bf16), peer_keys[0].astype(f32),
                peer_u[0].astype(bf16), peer_v[0].astype(bf16).T, g2, nw[3:4],
                tokens_per_batch=L, tt=_pick(L, (256, 128)), ec=1024)
    return out.reshape(B, L, D)
```

```python
import functools
import math

import jax
import jax.numpy as jnp
from jax import lax
from jax.experimental import pallas as pl
from jax.experimental.pallas import tpu as pltpu

f32 = jnp.float32
bf16 = jnp.bfloat16

EPS = 1e-6
GRID_W = 64
N_DIR = 2
N_MOD = 6
ML_HEADS, ML_DQK, ML_DV = 4, 128, 256
DN_HEADS, DN_DK, DN_DV, DN_CONV = 8, 128, 128, 5
CHUNK = 64
PEER_HEADS, PEER_NKEYS, PEER_DQ, PEER_TOPK = 8, 128, 256, 16
LANE = 128
VMEM_LIMIT = 56 * 1024 * 1024

NT = (((1,), (1,)), ((), ()))
TN = (((0,), (0,)), ((), ()))
NEG_INF = float("-inf")


def _params(sem):
    return pltpu.CompilerParams(dimension_semantics=sem, vmem_limit_bytes=VMEM_LIMIT)


def _mod_kernel(c_ref, w_ref, b_ref, o_ref):
    c = c_ref[...]
    s = c * jax.nn.sigmoid(c)
    o_ref[...] = jnp.dot(s, w_ref[...], preferred_element_type=f32) + b_ref[...]


def _modulation(cond, w_mod, b_mod):
    n, d = cond.shape
    return pl.pallas_call(
        _mod_kernel,
        name="modulation",
        out_shape=jax.ShapeDtypeStruct((n, N_MOD * d), f32),
        grid=(N_MOD,),
        in_specs=[pl.BlockSpec((n, d), lambda j: (0, 0)),
                  pl.BlockSpec((d, d), lambda j: (0, j)),
                  pl.BlockSpec((1, d), lambda j: (0, j))],
        out_specs=pl.BlockSpec((n, d), lambda j: (0, j)),
        compiler_params=_params(("arbitrary",)),
    )(cond, w_mod, b_mod.reshape(1, -1))


def _proj_kernel(x_ref, nw_ref, sc_ref, sh_ref, w_ref, o_ref, *, head_major):
    x = x_ref[0]
    ms = jnp.mean(x * x, axis=-1, keepdims=True)
    h = (x * lax.rsqrt(ms + EPS)) * nw_ref[...]
    h = h * (1.0 + sc_ref[0]) + sh_ref[0]
    acc = jnp.dot(h.astype(bf16), w_ref[...], preferred_element_type=f32)
    if head_major:
        for i in range(acc.shape[1] // LANE):
            o_ref[0, i] = acc[:, i * LANE:(i + 1) * LANE]
    else:
        o_ref[0] = acc


def _project(xs, nw, sc, sh, w, *, tn, tt, head_major):
    B, L, D = xs.shape
    N = w.shape[1]
    per_batch = sc.shape[0] == B and B > 1
    mod_map = (lambda j, b, t: (b, 0, 0)) if per_batch else (lambda j, b, t: (0, 0, 0))
    if head_major:
        out_shape = jax.ShapeDtypeStruct((B, N // LANE, L, LANE), f32)
        out_spec = pl.BlockSpec((1, tn // LANE, tt, LANE), lambda j, b, t: (b, j, t, 0))
    else:
        out_shape = jax.ShapeDtypeStruct((B, L, N), f32)
        out_spec = pl.BlockSpec((1, tt, tn), lambda j, b, t: (b, t, j))
    return pl.pallas_call(
        functools.partial(_proj_kernel, head_major=head_major),
        name="in_proj",
        out_shape=out_shape,
        grid=(N // tn, B, L // tt),
        in_specs=[pl.BlockSpec((1, tt, D), lambda j, b, t: (b, t, 0)),
                  pl.BlockSpec((1, D), lambda j, b, t: (0, 0)),
                  pl.BlockSpec((1, 1, D), mod_map),
                  pl.BlockSpec((1, 1, D), mod_map),
                  pl.BlockSpec((D, tn), lambda j, b, t: (0, j))],
        out_specs=out_spec,
        compiler_params=_params(("arbitrary", "arbitrary", "arbitrary")),
    )(xs, nw, sc, sh, w)


def _chunk_masks(reverse, n=1):
    size = n * CHUNK
    row = lax.broadcasted_iota(jnp.int32, (size, size), 0)
    col = lax.broadcasted_iota(jnp.int32, (size, size), 1)
    eye = row == col
    same = (row >> 6) == (col >> 6) if n > 1 else None
    both = (lambda m: jnp.logical_and(same, m)) if n > 1 else (lambda m: m)
    if reverse:
        return eye, both(col >= row), both(row >= col), same
    return eye, both(col <= row), both(row <= col), same


def _row_to_col(v_row, eye):
    return jnp.sum(jnp.where(eye, jnp.broadcast_to(v_row, eye.shape), 0.0), axis=1, keepdims=True)


def _cumsum_forms(v_row, eye, incl, incl_t):
    vb = jnp.broadcast_to(v_row, eye.shape)
    c_col = jnp.sum(jnp.where(incl, vb, 0.0), axis=1, keepdims=True)
    v_col = jnp.sum(jnp.where(eye, vb, 0.0), axis=1, keepdims=True)
    c_row = jnp.sum(jnp.where(incl_t, v_col, 0.0), axis=0, keepdims=True)
    return c_col, c_row


def _ml_chunks(problems, c_ref, n_ref, want_out):
    dirs = range(len(problems))
    gate = []
    for qs, k, v, ig_row, lf_row, m, masks in problems:
        eye, incl, incl_t, _ = masks
        b_col, b_row = _cumsum_forms(lf_row, eye, incl, incl_t)
        b_last = jnp.sum(lf_row, axis=1, keepdims=True)
        g_row = b_last - b_row + ig_row
        m_chunk = jnp.max(g_row, axis=1, keepdims=True)
        e_col = _row_to_col(jnp.exp(g_row - m_chunk), eye)
        ek = k * e_col
        gate.append((b_col, b_row, b_last, m_chunk, ek))
    v16 = [p[2].astype(bf16) for p in problems]
    kv = [lax.dot_general(gate[d][4].astype(bf16), v16[d], TN, preferred_element_type=f32) for d in dirs]
    c_old = [c_ref[d] for d in dirs]
    n_old = [n_ref[d] for d in dirs]
    hs = [None for _ in dirs]
    if want_out:
        q16 = [p[0].astype(bf16) for p in problems]
        qk = [lax.dot_general(q16[d], problems[d][1].astype(bf16), NT, preferred_element_type=f32) for d in dirs]
        qc = [jnp.dot(q16[d], c_old[d].astype(bf16), preferred_element_type=f32) for d in dirs]
        w_in, m_in = [], []
        for d in dirs:
            b_col, b_row = gate[d][0], gate[d][1]
            dlog = jnp.where(problems[d][6][1], b_col - b_row + problems[d][3], NEG_INF)
            mi = jnp.max(dlog, axis=1, keepdims=True)
            m_in.append(mi)
            w_in.append(jnp.exp(dlog - mi) * qk[d])
        num_in = [jnp.dot(w_in[d].astype(bf16), v16[d], preferred_element_type=f32) for d in dirs]
        for d in dirs:
            qs, m = problems[d][0], problems[d][5]
            den_in = jnp.sum(w_in[d], axis=1, keepdims=True)
            m_inter = gate[d][0] + m
            m_t = jnp.maximum(m_inter, m_in[d])
            a = jnp.exp(m_inter - m_t)
            r = jnp.exp(m_in[d] - m_t)
            num = a * qc[d] + r * num_in[d]
            den = a * jnp.sum(qs * n_old[d], axis=1, keepdims=True) + r * den_in
            hs[d] = num / jnp.maximum(jnp.abs(den), jnp.exp(-m_t))
    ms = []
    for d in dirs:
        _, _, b_last, m_chunk, ek = gate[d]
        m = problems[d][5]
        m_new = jnp.maximum(b_last + m, m_chunk)
        sp = jnp.exp(b_last + m - m_new)
        sc = jnp.exp(m_chunk - m_new)
        c_ref[d] = sp * c_old[d] + sc * kv[d]
        n_ref[d] = sp * n_old[d] + sc * jnp.sum(ek, axis=0, keepdims=True)
        ms.append(m_new)
    return hs, ms


def _mlstm_kernel(q_ref, k_ref, v_ref, kc_ref, vc_ref, gi_ref, gf_ref, gic_ref, gfc_ref, bias_ref,
                  o_ref, c_ref, n_ref):
    n_lat = q_ref.shape[1] // CHUNK
    n_ctx = kc_ref.shape[1] // CHUNK
    scale = ML_DQK ** -0.5
    c_ref[...] = jnp.zeros_like(c_ref)
    n_ref[...] = jnp.zeros_like(n_ref)
    o_ref[...] = jnp.zeros_like(o_ref)
    masks = [_chunk_masks(False), _chunk_masks(True)]

    def gates(i_ref, f_ref, d, c):
        ig = i_ref[0, d, 0, pl.ds(c, 1), :] + bias_ref[0, d, 0]
        lf = jax.nn.log_sigmoid(f_ref[0, d, 0, pl.ds(c, 1), :] + bias_ref[1, d, 0])
        return ig, lf

    def ctx_body(i, ms):
        problems = []
        for d in range(N_DIR):
            c = i if d == 0 else n_ctx - 1 - i
            sl = pl.ds(pl.multiple_of(c * CHUNK, CHUNK), CHUNK)
            ig, lf = gates(gic_ref, gfc_ref, d, c)
            problems.append((None, kc_ref[0, sl, :], vc_ref[0, sl, :], ig, lf, ms[d], masks[d]))
        _, out = _ml_chunks(problems, c_ref, n_ref, False)
        return tuple(out)

    def lat_body(i, ms):
        problems, slices = [], []
        for d in range(N_DIR):
            c = i if d == 0 else n_lat - 1 - i
            sl = pl.ds(pl.multiple_of(c * CHUNK, CHUNK), CHUNK)
            ig, lf = gates(gi_ref, gf_ref, d, c)
            problems.append((q_ref[0, sl, :] * scale, k_ref[0, sl, :], v_ref[0, sl, :], ig, lf, ms[d], masks[d]))
            slices.append(sl)
        hs, out = _ml_chunks(problems, c_ref, n_ref, True)
        for d in range(N_DIR):
            o_ref[0, slices[d], :] += hs[d]
        return tuple(out)

    m0 = (jnp.zeros((1, 1), f32), jnp.zeros((1, 1), f32))
    ms = lax.fori_loop(0, n_ctx, ctx_body, m0)
    lax.fori_loop(0, n_lat, lat_body, ms)


def _mlstm(p_lat, p_ctx, gi, gf, gic, gfc, bias):
    B, L, _ = p_lat.shape
    Lc = p_ctx.shape[1]
    n_lat, n_ctx = L // CHUNK, Lc // CHUNK
    qk_blocks = ML_HEADS * ML_DQK // ML_DQK
    v_blocks = 2 * ML_HEADS * ML_DQK // ML_DV
    gate_spec = lambda n: pl.BlockSpec((1, N_DIR, 1, n, CHUNK), lambda b, h: (b, 0, h, 0, 0))
    return pl.pallas_call(
        _mlstm_kernel,
        name="mlstm_scan",
        out_shape=jax.ShapeDtypeStruct((B, L, ML_HEADS * ML_DV), f32),
        grid=(B, ML_HEADS),
        in_specs=[pl.BlockSpec((1, L, ML_DQK), lambda b, h: (b, 0, h)),
                  pl.BlockSpec((1, L, ML_DQK), lambda b, h: (b, 0, qk_blocks + h)),
                  pl.BlockSpec((1, L, ML_DV), lambda b, h: (b, 0, v_blocks + h)),
                  pl.BlockSpec((1, Lc, ML_DQK), lambda b, h: (b, 0, qk_blocks + h)),
                  pl.BlockSpec((1, Lc, ML_DV), lambda b, h: (b, 0, v_blocks + h)),
                  gate_spec(n_lat), gate_spec(n_lat), gate_spec(n_ctx), gate_spec(n_ctx),
                  pl.BlockSpec((2, N_DIR, 1, 1, CHUNK), lambda b, h: (0, 0, h, 0, 0))],
        out_specs=pl.BlockSpec((1, L, ML_DV), lambda b, h: (b, 0, h)),
        scratch_shapes=[pltpu.VMEM((N_DIR, ML_DQK, ML_DV), f32), pltpu.VMEM((N_DIR, 1, ML_DQK), f32)],
        compiler_params=_params(("arbitrary", "arbitrary")),
    )(p_lat, p_lat, p_lat, p_ctx, p_ctx, gi, gf, gic, gfc, bias)


def _dn_prep_kernel(x_ref, w_ref, o_ref, pad_ref, *, seq_len, n_cols):
    kind = pl.program_id(1) // DN_HEADS
    half = DN_CONV // 2
    width = n_cols * LANE
    pad_ref[0:8, :] = jnp.zeros((8, width), f32)
    pad_ref[8 + seq_len:16 + seq_len, :] = jnp.zeros((8, width), f32)
    pad_ref[8:8 + seq_len, :] = x_ref[0, 0]
    y = jnp.zeros((seq_len, width), f32)
    for t in range(DN_CONV):
        y = y + pad_ref[pl.ds(8 - half + t, seq_len), :] * w_ref[0, t:t + 1, :]
    y = y * jax.nn.sigmoid(y)
    q_scale = jnp.where(kind == 0, DN_DK ** -0.5, 1.0).astype(f32)
    is_v = kind == 2
    for c in range(n_cols):
        seg = y[:, c * LANE:(c + 1) * LANE]
        ss = jnp.sum(seg * seg, axis=-1, keepdims=True)
        fac = jnp.where(is_v, 1.0, lax.rsqrt(ss + EPS) * q_scale)
        seg = seg * fac
        if seq_len >= CHUNK:
            for i in range(seq_len // CHUNK):
                o_ref[0, 0, i] = seg[i * CHUNK:(i + 1) * CHUNK]
        else:
            per = CHUNK // seq_len
            o_ref[0, 0, c // per, (c % per) * seq_len:(c % per + 1) * seq_len, :] = seg


def _dn_prep(raw, wt, *, seq_len, n_cols):
    B, G, L, _ = raw.shape
    n_chunks = L // CHUNK
    view = raw.reshape(B, G, seq_len, n_cols * LANE)
    return pl.pallas_call(
        functools.partial(_dn_prep_kernel, seq_len=seq_len, n_cols=n_cols),
        name="gdn_prep",
        out_shape=jax.ShapeDtypeStruct((B, G, n_chunks, CHUNK, LANE), f32),
        grid=(B, G),
        in_specs=[pl.BlockSpec((1, 1, seq_len, n_cols * LANE), lambda b, g: (b, g, 0, 0)),
                  pl.BlockSpec((1, 8, n_cols * LANE), lambda b, g: (g, 0, 0))],
        out_specs=pl.BlockSpec((1, 1, n_chunks, CHUNK, LANE), lambda b, g: (b, g, 0, 0, 0)),
        scratch_shapes=[pltpu.VMEM((seq_len + 16, n_cols * LANE), f32)],
        compiler_params=_params(("arbitrary", "arbitrary")),
    )(view, wt)


def _mm16(a, b):
    return jnp.dot(a.astype(bf16), b.astype(bf16), preferred_element_type=f32)


def _unit_tri_inverses(mats, eye):
    row = lax.broadcasted_iota(jnp.int32, eye.shape, 0)
    col = lax.broadcasted_iota(jnp.int32, eye.shape, 1)
    same = lambda bits: (row >> bits) == (col >> bits)
    m8 = [jnp.where(same(3), m, 0.0) for m in mats]
    p2 = [_mm16(m, m) for m in m8]
    p4 = [_mm16(p, p) for p in p2]
    inv = [eye.astype(f32) - m for m in m8]
    inv = [i + _mm16(i, p) for i, p in zip(inv, p2)]
    inv = [i + _mm16(i, p) for i, p in zip(inv, p4)]
    for bits in (3, 4, 5):
        joining = jnp.logical_and(same(bits + 1), jnp.logical_not(same(bits)))
        right = [_mm16(jnp.where(joining, m, 0.0), i) for m, i in zip(mats, inv)]
        inv = [i - _mm16(i, r) for i, r in zip(inv, right)]
    return inv


def _dn_prepare(problems):
    n = len(problems)
    pre = []
    for q, k, v, braw_row, araw_row, a_scale, dt_bias, masks in problems:
        eye, incl, incl_t, same = masks
        beta_col = _row_to_col(jax.nn.sigmoid(braw_row), eye)
        g_row = a_scale * jax.nn.softplus(araw_row + dt_bias)
        gb = jnp.broadcast_to(g_row, eye.shape)
        gc_col = jnp.sum(jnp.where(incl, gb, 0.0), axis=1, keepdims=True)
        g_col = jnp.sum(jnp.where(eye, gb, 0.0), axis=1, keepdims=True)
        gc_row = jnp.sum(jnp.where(incl_t, g_col, 0.0), axis=0, keepdims=True)
        if same is None:
            g_last = jnp.sum(g_row, axis=1, keepdims=True)
        else:
            g_last = jnp.sum(jnp.where(same, gb, 0.0), axis=1, keepdims=True)
        gam = jnp.exp(jnp.where(incl, gc_col - gc_row, NEG_INF))
        pre.append((beta_col, gc_col, g_last, gam, k * beta_col, k.astype(bf16)))
    eye = problems[0][7][0]
    gram = [lax.dot_general(pre[i][4].astype(bf16), pre[i][5], NT, preferred_element_type=f32) for i in range(n)]
    mats = []
    for i in range(n):
        _, incl, _, _ = problems[i][7]
        mats.append(jnp.where(jnp.logical_and(incl, jnp.logical_not(eye)), gram[i] * pre[i][3], 0.0))
    inv = [m.astype(bf16) for m in _unit_tri_inverses(mats, eye)]
    egc = [jnp.exp(pre[i][1]) for i in range(n)]
    u = [jnp.dot(inv[i], (problems[i][2] * pre[i][0]).astype(bf16), preferred_element_type=f32) for i in range(n)]
    w = [jnp.dot(inv[i], (pre[i][4] * egc[i]).astype(bf16), preferred_element_type=f32) for i in range(n)]
    a_qk = [None if problems[i][0] is None else
            lax.dot_general(problems[i][0].astype(bf16), pre[i][5], NT, preferred_element_type=f32) * pre[i][3]
            for i in range(n)]
    out = []
    for i in range(n):
        q, k = problems[i][0], problems[i][1]
        res = [u[i], w[i].astype(bf16), k * jnp.exp(pre[i][2] - pre[i][1]), jnp.exp(pre[i][2])]
        if q is not None:
            res += [(q * egc[i]).astype(bf16), a_qk[i].astype(bf16)]
        out.append(res)
    return out


def _gdn_kernel(q_ref, k_ref, v_ref, kc_ref, vc_ref, be_ref, ar_ref, bec_ref, arc_ref, alog_ref, dtb_ref,
                o_ref, s_ref, acc_ref, u_ref, w_ref, kd_ref, dl_ref, qg_ref, aq_ref, *, rows, group):
    n_lat = q_ref.shape[2]
    n_ctx = kc_ref.shape[2]
    size = group * CHUNK
    masks = [_chunk_masks(False, group), _chunk_masks(True, group)]

    def prepare(blocks, base, qr, kr, vr, ber, arr):
        problems = []
        for i in blocks:
            blk = pl.ds(i * group, group)
            q = None if qr is None else qr[0, 0, blk].reshape(size, DN_DK)
            k = kr[0, 0, blk].reshape(size, DN_DK)
            v = vr[0, 0, blk].reshape(size, DN_DV)
            for d in range(N_DIR):
                problems.append((q, k, v, ber[0, d, 0, pl.ds(i, 1), :], arr[0, d, 0, pl.ds(i, 1), :],
                                 -jnp.exp(alog_ref[d, 0]), dtb_ref[d, 0], masks[d]))
        results = _dn_prepare(problems)
        for n, res in enumerate(results):
            i, d = blocks[n // N_DIR], n % N_DIR
            dst = pl.ds(base + i * group, group)
            u_ref[d, dst] = res[0].reshape(group, CHUNK, DN_DV)
            w_ref[d, dst] = res[1].reshape(group, CHUNK, DN_DK)
            dl = jnp.broadcast_to(res[3], (size, LANE))
            for g in range(group):
                c = base + i * group + g
                kd_ref[d, c] = res[2][g * CHUNK:(g + 1) * CHUNK].T.astype(bf16)
                dl_ref[d, c] = dl[g * CHUNK:g * CHUNK + 8]
            if qr is not None:
                qg_ref[d, pl.ds(i * group, group)] = res[4].reshape(group, CHUNK, DN_DK)
                for g in range(group):
                    aq_ref[d, i * group + g] = res[5][g * CHUNK:(g + 1) * CHUNK, g * CHUNK:(g + 1) * CHUNK]

    def prep_loop(n_blocks, base, qr, kr, vr, ber, arr):
        per_iter = 2 if n_blocks % 2 == 0 else 1

        def body(i, carry):
            prepare([i * per_iter + j for j in range(per_iter)], base, qr, kr, vr, ber, arr)
            return carry

        lax.fori_loop(0, n_blocks // per_iter, body, 0)

    prep_loop(n_ctx // group, 0, None, kc_ref, vc_ref, bec_ref, arc_ref)
    prep_loop(n_lat // group, n_ctx, q_ref, k_ref, v_ref, be_ref, ar_ref)

    s_ref[...] = jnp.zeros_like(s_ref)
    acc_ref[...] = jnp.zeros_like(acc_ref)

    def step(cs, cls):
        dirs = range(N_DIR)
        s_old = [s_ref[d] for d in dirs]
        s16 = [s.astype(bf16) for s in s_old]
        ws = [jnp.dot(w_ref[d, cs[d]], s16[d], preferred_element_type=f32) for d in dirs]
        vn16 = [(u_ref[d, cs[d]] - ws[d]).astype(bf16) for d in dirs]
        upd = [jnp.dot(kd_ref[d, cs[d]], vn16[d], preferred_element_type=f32) for d in dirs]
        for d in dirs:
            s_ref[d] = dl_ref[d, cs[d]][0:1, :] * s_old[d] + upd[d]
        if cls is not None:
            inter = [jnp.dot(qg_ref[d, cls[d]], s16[d], preferred_element_type=f32) for d in dirs]
            intra = [jnp.dot(aq_ref[d, cls[d]], vn16[d], preferred_element_type=f32) for d in dirs]
            for d in dirs:
                acc_ref[cls[d]] += inter[d] + intra[d]

    def ctx_step(i, carry):
        step((i, n_ctx - 1 - i), None)
        return carry

    def lat_step(i, carry):
        step((n_ctx + i, n_ctx + n_lat - 1 - i), (i, n_lat - 1 - i))
        return carry

    lax.fori_loop(0, n_ctx, ctx_step, 0)
    lax.fori_loop(0, n_lat, lat_step, 0)
    per = CHUNK // rows
    for c in range(GRID_W):
        o_ref[0, 0, :, c * LANE:(c + 1) * LANE] = acc_ref[c // per, (c % per) * rows:(c % per + 1) * rows, :]


def _gdn(qkv_lat, qkv_ctx, be, ar, bec, arc, alog, dtb, *, rows):
    B, _, n_lat, _, _ = qkv_lat.shape
    n_ctx = qkv_ctx.shape[2]
    L = n_lat * CHUNK
    H = DN_HEADS
    n_all = n_ctx + n_lat
    width = be.shape[-1]
    group = width // CHUNK
    blk = lambda n, off: pl.BlockSpec((1, 1, n, CHUNK, LANE), lambda b, h: (b, off + h, 0, 0, 0))
    gate_spec = lambda n: pl.BlockSpec((1, N_DIR, 1, n // group, width), lambda b, h: (b, 0, h, 0, 0))
    const_spec = pl.BlockSpec((N_DIR, 1, 1, width), lambda b, h: (0, h, 0, 0))
    out = pl.pallas_call(
        functools.partial(_gdn_kernel, rows=rows, group=group),
        name="gdn_scan",
        out_shape=jax.ShapeDtypeStruct((B, H, rows, GRID_W * LANE), f32),
        grid=(B, H),
        in_specs=[blk(n_lat, 0), blk(n_lat, H), blk(n_lat, 2 * H), blk(n_ctx, H), blk(n_ctx, 2 * H),
                  gate_spec(n_lat), gate_spec(n_lat), gate_spec(n_ctx), gate_spec(n_ctx),
                  const_spec, const_spec],
        out_specs=pl.BlockSpec((1, 1, rows, GRID_W * LANE), lambda b, h: (b, h, 0, 0)),
        scratch_shapes=[pltpu.VMEM((N_DIR, DN_DK, DN_DV), f32), pltpu.VMEM((n_lat, CHUNK, DN_DV), f32),
                        pltpu.VMEM((N_DIR, n_all, CHUNK, DN_DV), f32),
                        pltpu.VMEM((N_DIR, n_all, CHUNK, DN_DK), bf16),
                        pltpu.VMEM((N_DIR, n_all, DN_DK, CHUNK), bf16),
                        pltpu.VMEM((N_DIR, n_all, 8, LANE), f32),
                        pltpu.VMEM((N_DIR, n_lat, CHUNK, DN_DK), bf16),
                        pltpu.VMEM((N_DIR, n_lat, CHUNK, CHUNK), bf16)],
        compiler_params=_params(("arbitrary", "arbitrary")),
    )(qkv_lat, qkv_lat, qkv_lat, qkv_ctx, qkv_ctx, be, ar, bec, arc, alog, dtb)
    return out.reshape(B, H, L, LANE)


def _merge_kernel(x_ref, hm_ref, hd_ref, o_ref_in, z_ref, ga_ref, gb_ref, mlw_ref, dnw_ref, wa_ref, wb_ref,
                  wo_ref, nw1_ref, nw2_ref, g1_ref, sc2_ref, sh2_ref, lat_ref, h2_ref):
    hm = hm_ref[0]
    parts = []
    for h in range(ML_HEADS):
        seg = hm[:, h * ML_DV:(h + 1) * ML_DV]
        ms = jnp.mean(seg * seg, axis=-1, keepdims=True)
        parts.append(seg * lax.rsqrt(ms + EPS))
    ym = jnp.concatenate(parts, axis=-1) * mlw_ref[...] * jax.nn.sigmoid(o_ref_in[0])
    parts = []
    for h in range(DN_HEADS):
        seg = hd_ref[0, h]
        ms = jnp.mean(seg * seg, axis=-1, keepdims=True)
        parts.append(seg * lax.rsqrt(ms + EPS))
    z = z_ref[0]
    yd = jnp.concatenate(parts, axis=-1) * dnw_ref[...] * (z * jax.nn.sigmoid(z))
    ya = jnp.dot(ym.astype(bf16), wa_ref[...], preferred_element_type=f32)
    yb = jnp.dot(yd.astype(bf16), wb_ref[...], preferred_element_type=f32)
    y = jax.nn.sigmoid(ga_ref[0]) * ya + jax.nn.sigmoid(gb_ref[0]) * yb
    ymix = jnp.dot(y.astype(bf16), wo_ref[...], preferred_element_type=f32)
    ms = jnp.mean(ymix * ymix, axis=-1, keepdims=True)
    lat = x_ref[0] + g1_ref[0] * (ymix * lax.rsqrt(ms + EPS) * nw1_ref[...])
    lat_ref[0] = lat
    ms = jnp.mean(lat * lat, axis=-1, keepdims=True)
    h2 = (lat * lax.rsqrt(ms + EPS) * nw2_ref[...]) * (1.0 + sc2_ref[0]) + sh2_ref[0]
    h2_ref[0] = h2.astype(bf16)


def _merge(x, hm, hd, p_lat, mlw, dnw, wa, wb, wo, nw1, nw2, g1, sc2, sh2, *, tt):
    B, L, D = x.shape
    tok = lambda off: pl.BlockSpec((1, tt, D), lambda b, t: (b, t, off))
    full = lambda a: pl.BlockSpec(a.shape, lambda b, t: (0,) * a.ndim)
    per_b = pl.BlockSpec((1, 1, D), lambda b, t: (b, 0, 0))
    return pl.pallas_call(
        _merge_kernel,
        name="merge",
        out_shape=(jax.ShapeDtypeStruct((B, L, D), f32), jax.ShapeDtypeStruct((B, L, D), bf16)),
        grid=(B, L // tt),
        in_specs=[tok(0), tok(0),
                  pl.BlockSpec((1, DN_HEADS, tt, LANE), lambda b, t: (b, 0, t, 0)),
                  tok(2), tok(3), tok(4), tok(5),
                  full(mlw), full(dnw), full(wa), full(wb), full(wo), full(nw1), full(nw2),
                  per_b, per_b, per_b],
        out_specs=(tok(0), tok(0)),
        compiler_params=_params(("arbitrary", "arbitrary")),
    )(x, hm, hd, p_lat, p_lat, p_lat, p_lat, mlw, dnw, wa, wb, wo, nw1, nw2, g1, sc2, sh2)


def _cand_pairs():
    return [(i, j) for i in range(PEER_TOPK) for j in range(PEER_TOPK) if (i + 1) * (j + 1) <= PEER_TOPK]


def _peer_select(h2_ref, wq_ref, keys_ref, rank_ref, f2_ref, cnt_ref, e1_ref, qh_ref, a_ref, b_ref, cand_ref):
    n_tiles = h2_ref.shape[0] // LANE
    pairs = _cand_pairs()
    half = PEER_DQ // 2

    def head_body(h, carry):
        qh_ref[...] = jnp.dot(h2_ref[...], wq_ref[h], preferred_element_type=f32)

        def tile_body(t, carry2):
            qt = qh_ref[pl.ds(pl.multiple_of(t * LANE, LANE), LANE), :].astype(bf16)
            s1 = lax.dot_general(keys_ref[h, 0], qt[:, :half], NT, preferred_element_type=f32)
            s2 = lax.dot_general(keys_ref[h, 1], qt[:, half:], NT, preferred_element_type=f32)
            work = s1
            for r in range(PEER_TOPK):
                mx = jnp.max(work, axis=0, keepdims=True)
                a_ref[r:r + 1, :] = mx
                if r + 1 < PEER_TOPK:
                    work = jnp.where(work == mx, NEG_INF, work)
            work = s2
            rank = jnp.full(s2.shape, float(PEER_TOPK), f32)
            for r in range(PEER_TOPK):
                mx = jnp.max(work, axis=0, keepdims=True)
                b_ref[r:r + 1, :] = mx
                hit = work == mx
                rank = jnp.where(hit, float(r), rank)
                if r + 1 < PEER_TOPK:
                    work = jnp.where(hit, NEG_INF, work)
            cand_ref[...] = jnp.full(cand_ref.shape, NEG_INF, f32)
            for c, (i, j) in enumerate(pairs):
                cand_ref[c:c + 1, :] = a_ref[i:i + 1, :] + b_ref[j:j + 1, :]
            cand = cand_ref[...]
            work = cand
            for r in range(PEER_TOPK):
                tau = jnp.max(work, axis=0, keepdims=True)
                if r + 1 < PEER_TOPK:
                    work = jnp.where(work == tau, NEG_INF, work)
            a0 = a_ref[0:1, :]
            b0 = b_ref[0:1, :]
            z = jnp.sum(jnp.where(cand >= tau, jnp.exp(cand - (a0 + b0)), 0.0), axis=0, keepdims=True)
            cnt = jnp.zeros(s1.shape, f32)
            for j in range(PEER_TOPK):
                cnt = cnt + jnp.where(s1 + b_ref[j:j + 1, :] >= tau, 1.0, 0.0)
            rank_ref[h, t] = rank.astype(bf16)
            f2_ref[h, t] = (jnp.exp(s2 - b0) / z).astype(bf16)
            cnt_ref[h, t] = cnt
            e1_ref[h, t] = jnp.exp(s1 - a0)
            return carry2

        return lax.fori_loop(0, n_tiles, tile_body, carry)

    lax.fori_loop(0, PEER_HEADS, head_body, 0)


def _peer_kernel(h2_ref, lat_ref, wq_ref, keys_ref, u_ref, vt_ref, g2_ref, nw_ref, o_ref,
                 rank_ref, f2_ref, cnt_ref, e1_ref, qh_ref, a_ref, b_ref, cand_ref, wg_ref, yt_ref):
    j = pl.program_id(1)
    n_blk = u_ref.shape[0] // PEER_NKEYS
    n_tiles = h2_ref.shape[0] // LANE
    n_parts = 2
    per = n_blk // n_parts

    @pl.when(j == 0)
    def _():
        _peer_select(h2_ref, wq_ref, keys_ref, rank_ref, f2_ref, cnt_ref, e1_ref, qh_ref, a_ref, b_ref, cand_ref)
        yt_ref[...] = jnp.zeros_like(yt_ref)

    at = lax.dot_general(u_ref[...], h2_ref[...], NT, preferred_element_type=f32)

    for t in range(n_tiles):
        for p in range(n_parts):
            ws = [None] * per
            for h in range(PEER_HEADS):
                rk = rank_ref[h, t]
                f2 = f2_ref[h, t]
                for b in range(per):
                    i1 = j * n_blk + p * per + b
                    cn = cnt_ref[h, t, pl.ds(i1, 1), :].astype(bf16)
                    ee = e1_ref[h, t, pl.ds(i1, 1), :].astype(bf16)
                    term = jnp.where(rk < cn, f2, 0) * ee
                    ws[b] = term if ws[b] is None else ws[b] + term
            for b in range(per):
                r0 = (p * per + b) * PEER_NKEYS
                wg_ref[r0:r0 + PEER_NKEYS, t * LANE:(t + 1) * LANE] = ws[b]

    y_step = None
    for p in range(n_parts):
        rows = slice(p * per * PEER_NKEYS, (p + 1) * per * PEER_NKEYS)
        a = at[rows, :]
        act = (0.5 * a * (1.0 + lax.erf(a * (2.0 ** -0.5)))).astype(bf16)
        wg = wg_ref[rows, :] * act
        part = jnp.dot(vt_ref[:, rows], wg, preferred_element_type=f32)
        y_step = part if y_step is None else y_step + part
    yt_ref[...] += y_step

    @pl.when(j == pl.num_programs(1) - 1)
    def _():
        y = yt_ref[...].T
        ms = jnp.mean(y * y, axis=-1, keepdims=True)
        o_ref[...] = lat_ref[...] + g2_ref[0] * (y * lax.rsqrt(ms + EPS) * nw_ref[...])


def _peer(h2, lat, wq, keys, u, vt, g2, nw, *, tokens_per_batch, tt, ec):
    N, D = h2.shape
    E = u.shape[0]
    tiles_per_batch = tokens_per_batch // tt
    sel = lambda dt: pltpu.VMEM((PEER_HEADS, tt // LANE, PEER_NKEYS, LANE), dt)
    return pl.pallas_call(
        _peer_kernel,
        name="peer",
        out_shape=jax.ShapeDtypeStruct((N, D), f32),
        grid=(N // tt, E // ec),
        in_specs=[pl.BlockSpec((tt, D), lambda i, j: (i, 0)),
                  pl.BlockSpec((tt, D), lambda i, j: (i, 0)),
                  pl.BlockSpec(wq.shape, lambda i, j: (0, 0, 0)),
                  pl.BlockSpec(keys.shape, lambda i, j: (0, 0, 0, 0)),
                  pl.BlockSpec((ec, D), lambda i, j: (j, 0)),
                  pl.BlockSpec((D, ec), lambda i, j: (0, j)),
                  pl.BlockSpec((1, 1, D), lambda i, j: (i // tiles_per_batch, 0, 0)),
                  pl.BlockSpec((1, D), lambda i, j: (0, 0))],
        out_specs=pl.BlockSpec((tt, D), lambda i, j: (i, 0)),
        scratch_shapes=[sel(bf16), sel(bf16), sel(f32), sel(f32),
                        pltpu.VMEM((tt, PEER_DQ), f32),
                        pltpu.VMEM((PEER_TOPK, LANE), f32),
                        pltpu.VMEM((PEER_TOPK, LANE), f32),
                        pltpu.VMEM((CHUNK, LANE), f32),
                        pltpu.VMEM((ec, tt), bf16),
                        pltpu.VMEM((D, tt), f32)],
        compiler_params=_params(("arbitrary", "arbitrary")),
    )(h2, lat, wq, keys, u, vt, g2, nw)


def _dir_rows(t, n_heads, width=CHUNK):
    B, L, _ = t.shape
    return t.reshape(B, L, N_DIR, n_heads).transpose(0, 2, 3, 1).reshape(B, N_DIR, n_heads, L // width, width)


def _to_colmajor(t, rows):
    B, _, C = t.shape
    return t.reshape(B, rows, GRID_W, C).transpose(0, 2, 1, 3).reshape(B, GRID_W * rows, C)


def _pick(n, prefs):
    for p in prefs:
        if n % p == 0:
            return p
    raise ValueError(f"no tile in {prefs} divides {n}")


def kernel(x, c, ctx, c_ctx, w_mod, b_mod, norm_w, w_in, ml_gate_bias, dn_a_log, dn_dt_bias, dn_conv_w,
           ml_norm_w, dn_norm_w, w_branch_a, w_branch_b, w_out, peer_wq, peer_keys, peer_u, peer_v):
    B, L, D = x.shape
    Lc = ctx.shape[1]
    assert w_mod.shape[0] == 1, "single-layer block"
    assert L % GRID_W == 0 and L % CHUNK == 0 and Lc % CHUNK == 0
    rows = L // GRID_W
    assert rows % 8 == 0 and CHUNK % rows == 0
    assert D == ML_HEADS * ML_DV == DN_HEADS * DN_DV

    mod = _modulation(jnp.concatenate([c, c_ctx[None, :]], axis=0), w_mod[0], b_mod[0])
    mod = mod.reshape(B + 1, N_MOD, 1, D)
    sh1, sc1, g1, sh2, sc2, g2 = (mod[:B, i] for i in range(N_MOD))
    sh1c, sc1c = mod[B:, 0], mod[B:, 1]

    w = w_in[0]
    sizes = (512, 512, 1024, 1024, 8, 8, 1024, 1024, 1024, 1024, 16, 16, 1024, 1024)
    offs = [0]
    for s in sizes:
        offs.append(offs[-1] + s)
    col = lambda i: w[:, offs[i]:offs[i + 1]]
    w_main = jnp.concatenate([col(0), col(1), col(2), col(3), col(9), col(12), col(13)], axis=1).astype(bf16)
    w_dn = jnp.concatenate([col(6), col(7), col(8)], axis=1).astype(bf16)
    w_gate = jnp.concatenate([col(4), col(5), col(10), col(11)], axis=1)
    n_gate = w_gate.shape[1]
    w_gate = jnp.pad(w_gate, ((0, 0), (0, LANE - n_gate))).astype(bf16)

    nw = norm_w[0]
    tt_lat = _pick(L, (512, 256, 128, 64))
    tt_ctx = _pick(Lc, (256, 128, 64))
    proj = lambda xs, sc_, sh_, tt: (
        _project(xs, nw[0:1], sc_, sh_, w_main, tn=1024, tt=tt, head_major=False),
        _project(xs, nw[0:1], sc_, sh_, w_dn, tn=1024, tt=tt, head_major=True),
        _project(xs, nw[0:1], sc_, sh_, w_gate, tn=LANE, tt=tt, head_major=False))
    p_lat, dn_lat, g_lat = proj(x, sc1, sh1, tt_lat)
    p_ctx, dn_ctx, g_ctx = proj(ctx, sc1c, sh1c, tt_ctx)

    nh = N_DIR * ML_HEADS
    nd = N_DIR * DN_HEADS
    gi, gf = _dir_rows(g_lat[..., 0:nh], ML_HEADS), _dir_rows(g_lat[..., nh:2 * nh], ML_HEADS)
    gic, gfc = _dir_rows(g_ctx[..., 0:nh], ML_HEADS), _dir_rows(g_ctx[..., nh:2 * nh], ML_HEADS)
    dn_width = CHUNK * math.gcd(math.gcd(L // CHUNK, Lc // CHUNK), 4)
    be = _dir_rows(_to_colmajor(g_lat[..., 2 * nh:2 * nh + nd], rows), DN_HEADS, dn_width)
    ar = _dir_rows(_to_colmajor(g_lat[..., 2 * nh + nd:2 * nh + 2 * nd], rows), DN_HEADS, dn_width)
    bec = _dir_rows(g_ctx[..., 2 * nh:2 * nh + nd], DN_HEADS, dn_width)
    arc = _dir_rows(g_ctx[..., 2 * nh + nd:2 * nh + 2 * nd], DN_HEADS, dn_width)
    bias = jnp.broadcast_to(ml_gate_bias[0].astype(f32)[:, :, :, None, None], (2, N_DIR, ML_HEADS, 1, CHUNK))
    alog = jnp.broadcast_to(dn_a_log[0].astype(f32)[:, :, None, None], (N_DIR, DN_HEADS, 1, dn_width))
    dtb = jnp.broadcast_to(dn_dt_bias[0].astype(f32)[:, :, None, None], (N_DIR, DN_HEADS, 1, dn_width))

    hm = _mlstm(p_lat, p_ctx, gi, gf, gic, gfc, bias)

    taps = dn_conv_w[0].astype(f32).reshape(DN_CONV, 3 * DN_HEADS, LANE).transpose(1, 0, 2)
    taps = jnp.pad(taps, ((0, 0), (0, 8 - DN_CONV), (0, 0)))
    qkv_lat = _dn_prep(dn_lat, jnp.tile(taps, (1, 1, GRID_W)), seq_len=rows, n_cols=GRID_W)
    qkv_ctx = _dn_prep(dn_ctx, taps, seq_len=Lc, n_cols=1)
    hd = _gdn(qkv_lat, qkv_ctx, be, ar, bec, arc, alog, dtb, rows=rows)

    lat1, h2 = _merge(x, hm, hd, p_lat, ml_norm_w[0].reshape(1, D).astype(f32),
                      jnp.tile(dn_norm_w[0].astype(f32), DN_HEADS).reshape(1, D),
                      w_branch_a[0].astype(bf16), w_branch_b[0].astype(bf16), w_out[0].astype(bf16),
                      nw[1:2], nw[2:3], g1, sc2, sh2, tt=_pick(L, (256, 128, 64)))

    wq = peer_wq[0].astype(bf16).reshape(D, PEER_HEADS, PEER_DQ).transpose(1, 0, 2)
    out = _peer(h2.reshape(B * L, D), lat1.reshape(B * L, D), wq, peer_keys[0].astype(bf16),
                peer_u[0].astype(bf16), peer_v[0].astype(bf16).T, g2, nw[3:4],
                tokens_per_batch=L, tt=_pick(L, (512, 256)), ec=1024)
    return out.reshape(B, L, D)
```

```python
import functools
import math

import jax
import jax.numpy as jnp
from jax import lax
from jax.experimental import pallas as pl
from jax.experimental.pallas import tpu as pltpu

f32 = jnp.float32
bf16 = jnp.bfloat16

EPS = 1e-6
GRID_W = 64
N_DIR = 2
N_MOD = 6
ML_HEADS, ML_DQK, ML_DV = 4, 128, 256
DN_HEADS, DN_DK, DN_DV, DN_CONV = 8, 128, 128, 5
CHUNK = 64
PEER_HEADS, PEER_NKEYS, PEER_DQ, PEER_TOPK = 8, 128, 256, 16
LANE = 128
VMEM_LIMIT = 56 * 1024 * 1024

NT = (((1,), (1,)), ((), ()))
TN = (((0,), (0,)), ((), ()))
NEG_INF = float("-inf")


def _params(sem, flags=None):
    return pltpu.CompilerParams(dimension_semantics=sem, vmem_limit_bytes=VMEM_LIMIT, flags=flags)


def _mod_kernel(c_ref, w_ref, b_ref, o_ref):
    c = c_ref[...]
    s = c * jax.nn.sigmoid(c)
    o_ref[...] = jnp.dot(s, w_ref[...], preferred_element_type=f32) + b_ref[...]


def _modulation(cond, w_mod, b_mod):
    n, d = cond.shape
    return pl.pallas_call(
        _mod_kernel,
        name="modulation",
        out_shape=jax.ShapeDtypeStruct((n, N_MOD * d), f32),
        grid=(N_MOD,),
        in_specs=[pl.BlockSpec((n, d), lambda j: (0, 0)),
                  pl.BlockSpec((d, d), lambda j: (0, j)),
                  pl.BlockSpec((1, d), lambda j: (0, j))],
        out_specs=pl.BlockSpec((n, d), lambda j: (0, j)),
        compiler_params=_params(("arbitrary",)),
    )(cond, w_mod, b_mod.reshape(1, -1))


def _proj_kernel(x_ref, nw_ref, sc_ref, sh_ref, w_ref, o_ref, h_ref, *, head_major):
    @pl.when(pl.program_id(2) == 0)
    def _():
        x = x_ref[0]
        ms = jnp.mean(x * x, axis=-1, keepdims=True)
        h = (x * lax.rsqrt(ms + EPS)) * nw_ref[...]
        h_ref[...] = (h * (1.0 + sc_ref[0]) + sh_ref[0]).astype(bf16)

    acc = jnp.dot(h_ref[...], w_ref[...], preferred_element_type=f32).astype(o_ref.dtype)
    if head_major:
        for i in range(acc.shape[1] // LANE):
            o_ref[0, i] = acc[:, i * LANE:(i + 1) * LANE]
    else:
        o_ref[0] = acc


def _project(xs, nw, sc, sh, w, *, tn, tt, head_major, out_dtype):
    B, L, D = xs.shape
    N = w.shape[1]
    per_batch = sc.shape[0] == B and B > 1
    mod_map = (lambda b, t, j: (b, 0, 0)) if per_batch else (lambda b, t, j: (0, 0, 0))
    if head_major:
        out_shape = jax.ShapeDtypeStruct((B, N // LANE, L, LANE), out_dtype)
        out_spec = pl.BlockSpec((1, tn // LANE, tt, LANE), lambda b, t, j: (b, j, t, 0))
    else:
        out_shape = jax.ShapeDtypeStruct((B, L, N), out_dtype)
        out_spec = pl.BlockSpec((1, tt, tn), lambda b, t, j: (b, t, j))
    return pl.pallas_call(
        functools.partial(_proj_kernel, head_major=head_major),
        name="in_proj",
        out_shape=out_shape,
        grid=(B, L // tt, N // tn),
        in_specs=[pl.BlockSpec((1, tt, D), lambda b, t, j: (b, t, 0)),
                  pl.BlockSpec((1, D), lambda b, t, j: (0, 0)),
                  pl.BlockSpec((1, 1, D), mod_map),
                  pl.BlockSpec((1, 1, D), mod_map),
                  pl.BlockSpec((D, tn), lambda b, t, j: (0, j))],
        out_specs=out_spec,
        scratch_shapes=[pltpu.VMEM((tt, D), bf16)],
        compiler_params=_params(("arbitrary", "arbitrary", "arbitrary")),
    )(xs, nw, sc, sh, w)


def _chunk_masks(reverse, n=1):
    size = n * CHUNK
    row = lax.broadcasted_iota(jnp.int32, (size, size), 0)
    col = lax.broadcasted_iota(jnp.int32, (size, size), 1)
    eye = row == col
    same = (row >> 6) == (col >> 6) if n > 1 else None
    both = (lambda m: jnp.logical_and(same, m)) if n > 1 else (lambda m: m)
    if reverse:
        return eye, both(col >= row), both(row >= col), same
    return eye, both(col <= row), both(row <= col), same


def _row_to_col(v_row, eye):
    return jnp.sum(jnp.where(eye, jnp.broadcast_to(v_row, eye.shape), 0.0), axis=1, keepdims=True)


def _cumsum_forms(v_row, eye, incl, incl_t):
    vb = jnp.broadcast_to(v_row, eye.shape)
    c_col = jnp.sum(jnp.where(incl, vb, 0.0), axis=1, keepdims=True)
    v_col = jnp.sum(jnp.where(eye, vb, 0.0), axis=1, keepdims=True)
    c_row = jnp.sum(jnp.where(incl_t, v_col, 0.0), axis=0, keepdims=True)
    return c_col, c_row


def _ml_chunks(problems, c_ref, n_ref, want_out):
    dirs = range(len(problems))
    gate = []
    for qs, k, v, ig_row, lf_row, m, masks in problems:
        eye, incl, incl_t, _ = masks
        b_col, b_row = _cumsum_forms(lf_row, eye, incl, incl_t)
        b_last = jnp.sum(lf_row, axis=1, keepdims=True)
        g_row = b_last - b_row + ig_row
        m_chunk = jnp.max(g_row, axis=1, keepdims=True)
        e_col = _row_to_col(jnp.exp(g_row - m_chunk), eye)
        ek = k * e_col
        gate.append((b_col, b_row, b_last, m_chunk, ek))
    v16 = [p[2].astype(bf16) for p in problems]
    kv = [lax.dot_general(gate[d][4].astype(bf16), v16[d], TN, preferred_element_type=f32) for d in dirs]
    c_old = [c_ref[d] for d in dirs]
    n_old = [n_ref[d] for d in dirs]
    hs = [None for _ in dirs]
    if want_out:
        q16 = [p[0].astype(bf16) for p in problems]
        qk = [lax.dot_general(q16[d], problems[d][1].astype(bf16), NT, preferred_element_type=f32) for d in dirs]
        qc = [jnp.dot(q16[d], c_old[d].astype(bf16), preferred_element_type=f32) for d in dirs]
        w_in, m_in = [], []
        for d in dirs:
            b_col, b_row = gate[d][0], gate[d][1]
            dlog = jnp.where(problems[d][6][1], b_col - b_row + problems[d][3], NEG_INF)
            mi = jnp.max(dlog, axis=1, keepdims=True)
            m_in.append(mi)
            w_in.append(jnp.exp(dlog - mi) * qk[d])
        num_in = [jnp.dot(w_in[d].astype(bf16), v16[d], preferred_element_type=f32) for d in dirs]
        for d in dirs:
            qs, m = problems[d][0], problems[d][5]
            den_in = jnp.sum(w_in[d], axis=1, keepdims=True)
            m_inter = gate[d][0] + m
            m_t = jnp.maximum(m_inter, m_in[d])
            a = jnp.exp(m_inter - m_t)
            r = jnp.exp(m_in[d] - m_t)
            num = a * qc[d] + r * num_in[d]
            den = a * jnp.sum(qs * n_old[d], axis=1, keepdims=True) + r * den_in
            hs[d] = num / jnp.maximum(jnp.abs(den), jnp.exp(-m_t))
    ms = []
    for d in dirs:
        _, _, b_last, m_chunk, ek = gate[d]
        m = problems[d][5]
        m_new = jnp.maximum(b_last + m, m_chunk)
        sp = jnp.exp(b_last + m - m_new)
        sc = jnp.exp(m_chunk - m_new)
        c_ref[d] = sp * c_old[d] + sc * kv[d]
        n_ref[d] = sp * n_old[d] + sc * jnp.sum(ek, axis=0, keepdims=True)
        ms.append(m_new)
    return hs, ms


def _mlstm_kernel(q_ref, k_ref, v_ref, kc_ref, vc_ref, gi_ref, gf_ref, gic_ref, gfc_ref, bias_ref,
                  o_ref, c_ref, n_ref):
    n_lat = q_ref.shape[1] // CHUNK
    n_ctx = kc_ref.shape[1] // CHUNK
    scale = ML_DQK ** -0.5
    c_ref[...] = jnp.zeros_like(c_ref)
    n_ref[...] = jnp.zeros_like(n_ref)
    o_ref[...] = jnp.zeros_like(o_ref)
    masks = [_chunk_masks(False), _chunk_masks(True)]

    n_heads = q_ref.shape[2] // ML_DQK
    chains = [(hh, d) for hh in range(n_heads) for d in range(N_DIR)]
    qk_cols = lambda hh: slice(hh * ML_DQK, (hh + 1) * ML_DQK)
    v_cols = lambda hh: slice(hh * ML_DV, (hh + 1) * ML_DV)

    def gates(i_ref, f_ref, hh, d, c):
        ig = i_ref[0, d, hh, pl.ds(c, 1), :] + bias_ref[0, d, hh]
        lf = jax.nn.log_sigmoid(f_ref[0, d, hh, pl.ds(c, 1), :] + bias_ref[1, d, hh])
        return ig, lf

    def ctx_body(i, ms):
        problems = []
        for n, (hh, d) in enumerate(chains):
            c = i if d == 0 else n_ctx - 1 - i
            sl = pl.ds(pl.multiple_of(c * CHUNK, CHUNK), CHUNK)
            ig, lf = gates(gic_ref, gfc_ref, hh, d, c)
            problems.append((None, kc_ref[0, sl, qk_cols(hh)].astype(f32), vc_ref[0, sl, v_cols(hh)], ig, lf,
                             ms[n], masks[d]))
        _, out = _ml_chunks(problems, c_ref, n_ref, False)
        return tuple(out)

    def lat_body(i, ms):
        problems, slices = [], []
        for n, (hh, d) in enumerate(chains):
            c = i if d == 0 else n_lat - 1 - i
            sl = pl.ds(pl.multiple_of(c * CHUNK, CHUNK), CHUNK)
            ig, lf = gates(gi_ref, gf_ref, hh, d, c)
            problems.append((q_ref[0, sl, qk_cols(hh)].astype(f32) * scale, k_ref[0, sl, qk_cols(hh)].astype(f32),
                             v_ref[0, sl, v_cols(hh)], ig, lf, ms[n], masks[d]))
            slices.append(sl)
        hs, out = _ml_chunks(problems, c_ref, n_ref, True)
        for n, (hh, d) in enumerate(chains):
            o_ref[0, slices[n], v_cols(hh)] += hs[n]
        return tuple(out)

    m0 = tuple(jnp.zeros((1, 1), f32) for _ in chains)
    ms = lax.fori_loop(0, n_ctx, ctx_body, m0)
    lax.fori_loop(0, n_lat, lat_body, ms)


def _mlstm(p_lat, p_ctx, gi, gf, gic, gfc, bias):
    B, L, _ = p_lat.shape
    Lc = p_ctx.shape[1]
    n_lat, n_ctx = L // CHUNK, Lc // CHUNK
    hp = 2
    qk_w, v_w = hp * ML_DQK, hp * ML_DV
    k_off = ML_HEADS * ML_DQK // qk_w
    v_off = 2 * ML_HEADS * ML_DQK // v_w
    gate_spec = lambda n: pl.BlockSpec((1, N_DIR, hp, n, CHUNK), lambda b, h: (b, 0, h, 0, 0))
    return pl.pallas_call(
        _mlstm_kernel,
        name="mlstm_scan",
        out_shape=jax.ShapeDtypeStruct((B, L, ML_HEADS * ML_DV), f32),
        grid=(B, ML_HEADS // hp),
        in_specs=[pl.BlockSpec((1, L, qk_w), lambda b, h: (b, 0, h)),
                  pl.BlockSpec((1, L, qk_w), lambda b, h: (b, 0, k_off + h)),
                  pl.BlockSpec((1, L, v_w), lambda b, h: (b, 0, v_off + h)),
                  pl.BlockSpec((1, Lc, qk_w), lambda b, h: (b, 0, k_off + h)),
                  pl.BlockSpec((1, Lc, v_w), lambda b, h: (b, 0, v_off + h)),
                  gate_spec(n_lat), gate_spec(n_lat), gate_spec(n_ctx), gate_spec(n_ctx),
                  pl.BlockSpec((2, N_DIR, hp, 1, CHUNK), lambda b, h: (0, 0, h, 0, 0))],
        out_specs=pl.BlockSpec((1, L, v_w), lambda b, h: (b, 0, h)),
        scratch_shapes=[pltpu.VMEM((hp * N_DIR, ML_DQK, ML_DV), f32), pltpu.VMEM((hp * N_DIR, 1, ML_DQK), f32)],
        compiler_params=_params(("arbitrary", "arbitrary")),
    )(p_lat, p_lat, p_lat, p_ctx, p_ctx, gi, gf, gic, gfc, bias)


def _dn_prep_kernel(x_ref, w_ref, o_ref, pad_ref, *, seq_len, n_cols):
    kind = pl.program_id(1) // DN_HEADS
    half = DN_CONV // 2
    width = n_cols * LANE
    pad_ref[0:8, :] = jnp.zeros((8, width), f32)
    pad_ref[8 + seq_len:16 + seq_len, :] = jnp.zeros((8, width), f32)
    pad_ref[8:8 + seq_len, :] = x_ref[0, 0]
    y = jnp.zeros((seq_len, width), f32)
    for t in range(DN_CONV):
        y = y + pad_ref[pl.ds(8 - half + t, seq_len), :] * w_ref[0, t:t + 1, :]
    y = y * jax.nn.sigmoid(y)
    q_scale = jnp.where(kind == 0, DN_DK ** -0.5, 1.0).astype(f32)
    is_v = kind == 2
    for c in range(n_cols):
        seg = y[:, c * LANE:(c + 1) * LANE]
        ss = jnp.sum(seg * seg, axis=-1, keepdims=True)
        fac = jnp.where(is_v, 1.0, lax.rsqrt(ss + EPS) * q_scale)
        seg = seg * fac
        if seq_len >= CHUNK:
            for i in range(seq_len // CHUNK):
                o_ref[0, 0, i] = seg[i * CHUNK:(i + 1) * CHUNK]
        else:
            per = CHUNK // seq_len
            o_ref[0, 0, c // per, (c % per) * seq_len:(c % per + 1) * seq_len, :] = seg


def _dn_prep(raw, wt, *, seq_len, n_cols):
    B, G, L, _ = raw.shape
    n_chunks = L // CHUNK
    view = raw.reshape(B, G, seq_len, n_cols * LANE)
    return pl.pallas_call(
        functools.partial(_dn_prep_kernel, seq_len=seq_len, n_cols=n_cols),
        name="gdn_prep",
        out_shape=jax.ShapeDtypeStruct((B, G, n_chunks, CHUNK, LANE), f32),
        grid=(B, G),
        in_specs=[pl.BlockSpec((1, 1, seq_len, n_cols * LANE), lambda b, g: (b, g, 0, 0)),
                  pl.BlockSpec((1, 8, n_cols * LANE), lambda b, g: (g, 0, 0))],
        out_specs=pl.BlockSpec((1, 1, n_chunks, CHUNK, LANE), lambda b, g: (b, g, 0, 0, 0)),
        scratch_shapes=[pltpu.VMEM((seq_len + 16, n_cols * LANE), f32)],
        compiler_params=_params(("arbitrary", "arbitrary")),
    )(view, wt)


def _mm16(a, b):
    return jnp.dot(a.astype(bf16), b.astype(bf16), preferred_element_type=f32)


def _unit_tri_inverses(mats, eye):
    row = lax.broadcasted_iota(jnp.int32, eye.shape, 0)
    col = lax.broadcasted_iota(jnp.int32, eye.shape, 1)
    same = lambda bits: (row >> bits) == (col >> bits)
    m8 = [jnp.where(same(3), m, 0.0) for m in mats]
    p2 = [_mm16(m, m) for m in m8]
    p4 = [_mm16(p, p) for p in p2]
    inv = [eye.astype(f32) - m for m in m8]
    inv = [i + _mm16(i, p) for i, p in zip(inv, p2)]
    inv = [i + _mm16(i, p) for i, p in zip(inv, p4)]
    for bits in (3, 4, 5):
        joining = jnp.logical_and(same(bits + 1), jnp.logical_not(same(bits)))
        right = [_mm16(jnp.where(joining, m, 0.0), i) for m, i in zip(mats, inv)]
        inv = [i - _mm16(i, r) for i, r in zip(inv, right)]
    return inv


def _dn_prepare(problems):
    n = len(problems)
    pre = []
    for q, k, v, braw_row, araw_row, a_scale, dt_bias, masks in problems:
        eye, incl, incl_t, same = masks
        beta_col = _row_to_col(jax.nn.sigmoid(braw_row), eye)
        g_row = a_scale * jax.nn.softplus(araw_row + dt_bias)
        gb = jnp.broadcast_to(g_row, eye.shape)
        gc_col = jnp.sum(jnp.where(incl, gb, 0.0), axis=1, keepdims=True)
        g_col = jnp.sum(jnp.where(eye, gb, 0.0), axis=1, keepdims=True)
        gc_row = jnp.sum(jnp.where(incl_t, g_col, 0.0), axis=0, keepdims=True)
        if same is None:
            g_last = jnp.sum(g_row, axis=1, keepdims=True)
        else:
            g_last = jnp.sum(jnp.where(same, gb, 0.0), axis=1, keepdims=True)
        gam = jnp.exp(jnp.where(incl, gc_col - gc_row, NEG_INF))
        pre.append((beta_col, gc_col, g_last, gam, k * beta_col, k.astype(bf16)))
    eye = problems[0][7][0]
    gram = [lax.dot_general(pre[i][4].astype(bf16), pre[i][5], NT, preferred_element_type=f32) for i in range(n)]
    mats = []
    for i in range(n):
        _, incl, _, _ = problems[i][7]
        mats.append(jnp.where(jnp.logical_and(incl, jnp.logical_not(eye)), gram[i] * pre[i][3], 0.0))
    inv = [m.astype(bf16) for m in _unit_tri_inverses(mats, eye)]
    egc = [jnp.exp(pre[i][1]) for i in range(n)]
    u = [jnp.dot(inv[i], (problems[i][2] * pre[i][0]).astype(bf16), preferred_element_type=f32) for i in range(n)]
    w = [jnp.dot(inv[i], (pre[i][4] * egc[i]).astype(bf16), preferred_element_type=f32) for i in range(n)]
    a_qk = [None if problems[i][0] is None else
            lax.dot_general(problems[i][0].astype(bf16), pre[i][5], NT, preferred_element_type=f32) * pre[i][3]
            for i in range(n)]
    out = []
    for i in range(n):
        q, k = problems[i][0], problems[i][1]
        res = [u[i], w[i].astype(bf16), k * jnp.exp(pre[i][2] - pre[i][1]), jnp.exp(pre[i][2])]
        if q is not None:
            res += [(q * egc[i]).astype(bf16), a_qk[i].astype(bf16)]
        out.append(res)
    return out


def _gdn_kernel(q_ref, k_ref, v_ref, kc_ref, vc_ref, be_ref, ar_ref, bec_ref, arc_ref, alog_ref, dtb_ref,
                o_ref, s_ref, acc_ref, u_ref, w_ref, kd_ref, dl_ref, qg_ref, aq_ref, *, rows, group):
    n_lat = q_ref.shape[2]
    n_ctx = kc_ref.shape[2]
    size = group * CHUNK
    masks = [_chunk_masks(False, group), _chunk_masks(True, group)]

    n_heads = k_ref.shape[1]
    chains = [(hh, d) for hh in range(n_heads) for d in range(N_DIR)]

    def prepare(i, base, qr, kr, vr, ber, arr):
        problems = []
        blk = pl.ds(i * group, group)
        for hh, d in chains:
            q = None if qr is None else qr[0, hh, blk].reshape(size, DN_DK)
            k = kr[0, hh, blk].reshape(size, DN_DK)
            v = vr[0, hh, blk].reshape(size, DN_DV)
            problems.append((q, k, v, ber[0, d, hh, pl.ds(i, 1), :], arr[0, d, hh, pl.ds(i, 1), :],
                             -jnp.exp(alog_ref[d, hh]), dtb_ref[d, hh], masks[d]))
        results = _dn_prepare(problems)
        for n, res in enumerate(results):
            dst = pl.ds(base + i * group, group)
            u_ref[n, dst] = res[0].reshape(group, CHUNK, DN_DV)
            w_ref[n, dst] = res[1].reshape(group, CHUNK, DN_DK)
            dl = jnp.broadcast_to(res[3], (size, LANE))
            for g in range(group):
                c = base + i * group + g
                kd_ref[n, c] = res[2][g * CHUNK:(g + 1) * CHUNK].T.astype(bf16)
                dl_ref[n, c] = dl[g * CHUNK:g * CHUNK + 8]
            if qr is not None:
                qg_ref[n, pl.ds(i * group, group)] = res[4].reshape(group, CHUNK, DN_DK)
                for g in range(group):
                    aq_ref[n, i * group + g] = res[5][g * CHUNK:(g + 1) * CHUNK, g * CHUNK:(g + 1) * CHUNK]

    def ctx_prep(i, carry):
        prepare(i, 0, None, kc_ref, vc_ref, bec_ref, arc_ref)
        return carry

    def lat_prep(i, carry):
        prepare(i, n_ctx, q_ref, k_ref, v_ref, be_ref, ar_ref)
        return carry

    lax.fori_loop(0, n_ctx // group, ctx_prep, 0)
    lax.fori_loop(0, n_lat // group, lat_prep, 0)

    s_ref[...] = jnp.zeros_like(s_ref)
    acc_ref[...] = jnp.zeros_like(acc_ref)

    def step(cs, cls):
        idx = range(len(chains))
        s_old = [s_ref[n] for n in idx]
        s16 = [s.astype(bf16) for s in s_old]
        ws = [jnp.dot(w_ref[n, cs[chains[n][1]]], s16[n], preferred_element_type=f32) for n in idx]
        vn16 = [(u_ref[n, cs[chains[n][1]]] - ws[n]).astype(bf16) for n in idx]
        upd = [jnp.dot(kd_ref[n, cs[chains[n][1]]], vn16[n], preferred_element_type=f32) for n in idx]
        for n in idx:
            s_ref[n] = dl_ref[n, cs[chains[n][1]]][0:1, :] * s_old[n] + upd[n]
        if cls is not None:
            inter = [jnp.dot(qg_ref[n, cls[chains[n][1]]], s16[n], preferred_element_type=f32) for n in idx]
            intra = [jnp.dot(aq_ref[n, cls[chains[n][1]]], vn16[n], preferred_element_type=f32) for n in idx]
            for n in idx:
                hh, d = chains[n]
                acc_ref[hh, cls[d]] += inter[n] + intra[n]

    def ctx_step(i, carry):
        step((i, n_ctx - 1 - i), None)
        return carry

    def lat_step(i, carry):
        step((n_ctx + i, n_ctx + n_lat - 1 - i), (i, n_lat - 1 - i))
        return carry

    lax.fori_loop(0, n_ctx, ctx_step, 0)
    lax.fori_loop(0, n_lat, lat_step, 0)
    per = CHUNK // rows
    for hh in range(n_heads):
        for c in range(GRID_W):
            o_ref[0, hh, :, c * LANE:(c + 1) * LANE] = acc_ref[hh, c // per, (c % per) * rows:(c % per + 1) * rows, :]


def _gdn(qkv_lat, qkv_ctx, be, ar, bec, arc, alog, dtb, *, rows):
    B, _, n_lat, _, _ = qkv_lat.shape
    n_ctx = qkv_ctx.shape[2]
    L = n_lat * CHUNK
    H = DN_HEADS
    n_all = n_ctx + n_lat
    width = be.shape[-1]
    group = width // CHUNK
    hp = 2
    nc = hp * N_DIR
    blk = lambda n, off: pl.BlockSpec((1, hp, n, CHUNK, LANE), lambda b, h: (b, off // hp + h, 0, 0, 0))
    gate_spec = lambda n: pl.BlockSpec((1, N_DIR, hp, n // group, width), lambda b, h: (b, 0, h, 0, 0))
    const_spec = pl.BlockSpec((N_DIR, hp, 1, width), lambda b, h: (0, h, 0, 0))
    out = pl.pallas_call(
        functools.partial(_gdn_kernel, rows=rows, group=group),
        name="gdn_scan",
        out_shape=jax.ShapeDtypeStruct((B, H, rows, GRID_W * LANE), f32),
        grid=(B, H // hp),
        in_specs=[blk(n_lat, 0), blk(n_lat, H), blk(n_lat, 2 * H), blk(n_ctx, H), blk(n_ctx, 2 * H),
                  gate_spec(n_lat), gate_spec(n_lat), gate_spec(n_ctx), gate_spec(n_ctx),
                  const_spec, const_spec],
        out_specs=pl.BlockSpec((1, hp, rows, GRID_W * LANE), lambda b, h: (b, h, 0, 0)),
        scratch_shapes=[pltpu.VMEM((nc, DN_DK, DN_DV), f32), pltpu.VMEM((hp, n_lat, CHUNK, DN_DV), f32),
                        pltpu.VMEM((nc, n_all, CHUNK, DN_DV), f32),
                        pltpu.VMEM((nc, n_all, CHUNK, DN_DK), bf16),
                        pltpu.VMEM((nc, n_all, DN_DK, CHUNK), bf16),
                        pltpu.VMEM((nc, n_all, 8, LANE), f32),
                        pltpu.VMEM((nc, n_lat, CHUNK, DN_DK), bf16),
                        pltpu.VMEM((nc, n_lat, CHUNK, CHUNK), bf16)],
        compiler_params=_params(("arbitrary", "arbitrary")),
    )(qkv_lat, qkv_lat, qkv_lat, qkv_ctx, qkv_ctx, be, ar, bec, arc, alog, dtb)
    return out.reshape(B, H, L, LANE)


def _merge_kernel(x_ref, hm_ref, hd_ref, o_ref_in, z_ref, ga_ref, gb_ref, mlw_ref, dnw_ref, wa_ref, wb_ref,
                  wo_ref, nw1_ref, nw2_ref, g1_ref, sc2_ref, sh2_ref, lat_ref, h2_ref):
    hm = hm_ref[0]
    parts = []
    for h in range(ML_HEADS):
        seg = hm[:, h * ML_DV:(h + 1) * ML_DV]
        ms = jnp.mean(seg * seg, axis=-1, keepdims=True)
        parts.append(seg * lax.rsqrt(ms + EPS))
    ym = jnp.concatenate(parts, axis=-1) * mlw_ref[...] * jax.nn.sigmoid(o_ref_in[0].astype(f32))
    parts = []
    for h in range(DN_HEADS):
        seg = hd_ref[0, h]
        ms = jnp.mean(seg * seg, axis=-1, keepdims=True)
        parts.append(seg * lax.rsqrt(ms + EPS))
    z = z_ref[0].astype(f32)
    yd = jnp.concatenate(parts, axis=-1) * dnw_ref[...] * (z * jax.nn.sigmoid(z))
    ya = jnp.dot(ym.astype(bf16), wa_ref[...], preferred_element_type=f32)
    yb = jnp.dot(yd.astype(bf16), wb_ref[...], preferred_element_type=f32)
    y = jax.nn.sigmoid(ga_ref[0].astype(f32)) * ya + jax.nn.sigmoid(gb_ref[0].astype(f32)) * yb
    ymix = jnp.dot(y.astype(bf16), wo_ref[...], preferred_element_type=f32)
    ms = jnp.mean(ymix * ymix, axis=-1, keepdims=True)
    lat = x_ref[0] + g1_ref[0] * (ymix * lax.rsqrt(ms + EPS) * nw1_ref[...])
    lat_ref[0] = lat
    ms = jnp.mean(lat * lat, axis=-1, keepdims=True)
    h2 = (lat * lax.rsqrt(ms + EPS) * nw2_ref[...]) * (1.0 + sc2_ref[0]) + sh2_ref[0]
    h2_ref[0] = h2.astype(bf16)


def _merge(x, hm, hd, p_lat, mlw, dnw, wa, wb, wo, nw1, nw2, g1, sc2, sh2, *, tt):
    B, L, D = x.shape
    tok = lambda off: pl.BlockSpec((1, tt, D), lambda b, t: (b, t, off))
    full = lambda a: pl.BlockSpec(a.shape, lambda b, t: (0,) * a.ndim)
    per_b = pl.BlockSpec((1, 1, D), lambda b, t: (b, 0, 0))
    return pl.pallas_call(
        _merge_kernel,
        name="merge",
        out_shape=(jax.ShapeDtypeStruct((B, L, D), f32), jax.ShapeDtypeStruct((B, L, D), bf16)),
        grid=(B, L // tt),
        in_specs=[tok(0), tok(0),
                  pl.BlockSpec((1, DN_HEADS, tt, LANE), lambda b, t: (b, 0, t, 0)),
                  tok(2), tok(3), tok(4), tok(5),
                  full(mlw), full(dnw), full(wa), full(wb), full(wo), full(nw1), full(nw2),
                  per_b, per_b, per_b],
        out_specs=(tok(0), tok(0)),
        compiler_params=_params(("arbitrary", "arbitrary")),
    )(x, hm, hd, p_lat, p_lat, p_lat, p_lat, mlw, dnw, wa, wb, wo, nw1, nw2, g1, sc2, sh2)


def _transpose_kernel(x_ref, o_ref):
    o_ref[...] = x_ref[...].T.astype(o_ref.dtype)


def _transposed_bf16(x, *, te):
    E, D = x.shape
    return pl.pallas_call(
        _transpose_kernel,
        name="table_transpose",
        out_shape=jax.ShapeDtypeStruct((D, E), bf16),
        grid=(E // te,),
        in_specs=[pl.BlockSpec((te, D), lambda i: (i, 0))],
        out_specs=pl.BlockSpec((D, te), lambda i: (0, i)),
        compiler_params=_params(("arbitrary",)),
    )(x)


def _cand_pairs():
    return [(i, j) for i in range(PEER_TOPK) for j in range(PEER_TOPK) if (i + 1) * (j + 1) <= PEER_TOPK]


def _peer_select(h2_ref, wq_ref, keys_ref, rank_ref, f2_ref, cnt_ref, e1_ref, qh_ref, a_ref, b_ref, cand_ref):
    n_tiles = h2_ref.shape[0] // LANE
    width = a_ref.shape[0]
    pairs = _cand_pairs()
    half = PEER_DQ // 2
    top = range(PEER_TOPK)

    def head_body(h, carry):
        qh_ref[...] = jnp.dot(h2_ref[...], wq_ref[h], preferred_element_type=f32)

        def tiles_body(g, carry2):
            lanes = range(width)
            tiles = [g * width + i for i in lanes]
            qt = [qh_ref[pl.ds(pl.multiple_of(t * LANE, LANE), LANE), :].astype(bf16) for t in tiles]
            s1 = [lax.dot_general(keys_ref[h, 0], q[:, :half], NT, preferred_element_type=f32) for q in qt]
            s2 = [lax.dot_general(keys_ref[h, 1], q[:, half:], NT, preferred_element_type=f32) for q in qt]
            w1, w2 = list(s1), list(s2)
            rank = [jnp.full(s.shape, float(PEER_TOPK), f32) for s in s2]
            for r in top:
                m1 = [jnp.max(w, axis=0, keepdims=True) for w in w1]
                m2 = [jnp.max(w, axis=0, keepdims=True) for w in w2]
                hit = [w2[i] == m2[i] for i in lanes]
                for i in lanes:
                    a_ref[i, r:r + 1, :] = m1[i]
                    b_ref[i, r:r + 1, :] = m2[i]
                rank = [jnp.where(hit[i], float(r), rank[i]) for i in lanes]
                if r + 1 < PEER_TOPK:
                    w1 = [jnp.where(w1[i] == m1[i], NEG_INF, w1[i]) for i in lanes]
                    w2 = [jnp.where(hit[i], NEG_INF, w2[i]) for i in lanes]
            cand_ref[...] = jnp.full(cand_ref.shape, NEG_INF, f32)
            for c, (i1, i2) in enumerate(pairs):
                for i in lanes:
                    cand_ref[i, c:c + 1, :] = a_ref[i, i1:i1 + 1, :] + b_ref[i, i2:i2 + 1, :]
            cand = [cand_ref[i] for i in lanes]
            work = list(cand)
            for r in top:
                tau = [jnp.max(w, axis=0, keepdims=True) for w in work]
                if r + 1 < PEER_TOPK:
                    work = [jnp.where(work[i] == tau[i], NEG_INF, work[i]) for i in lanes]
            a0 = [a_ref[i, 0:1, :] for i in lanes]
            b0 = [b_ref[i, 0:1, :] for i in lanes]
            z = [jnp.sum(jnp.where(cand[i] >= tau[i], jnp.exp(cand[i] - (a0[i] + b0[i])), 0.0), axis=0,
                         keepdims=True) for i in lanes]
            cnt = [jnp.zeros(s.shape, f32) for s in s1]
            for r in top:
                cnt = [cnt[i] + jnp.where(s1[i] + b_ref[i, r:r + 1, :] >= tau[i], 1.0, 0.0) for i in lanes]
            for i in lanes:
                rank_ref[h, tiles[i]] = rank[i].astype(bf16)
                f2_ref[h, tiles[i]] = (jnp.exp(s2[i] - b0[i]) * (0.5 / z[i])).astype(bf16)
                cnt_ref[h, tiles[i]] = cnt[i]
                e1_ref[h, tiles[i]] = jnp.exp(s1[i] - a0[i])
            return carry2

        return lax.fori_loop(0, n_tiles // width, tiles_body, carry)

    lax.fori_loop(0, PEER_HEADS, head_body, 0)


def _peer_kernel(h2_ref, lat_ref, wq_ref, keys_ref, u_ref, vt_ref, g2_ref, nw_ref, o_ref,
                 rank_ref, f2_ref, cnt_ref, e1_ref, qh_ref, a_ref, b_ref, cand_ref, wg_ref, yt_ref):
    j = pl.program_id(1)
    n_blk = u_ref.shape[0] // PEER_NKEYS
    n_tiles = h2_ref.shape[0] // LANE
    n_parts = 2
    per = n_blk // n_parts

    @pl.when(j == 0)
    def _():
        _peer_select(h2_ref, wq_ref, keys_ref, rank_ref, f2_ref, cnt_ref, e1_ref, qh_ref, a_ref, b_ref, cand_ref)
        yt_ref[...] = jnp.zeros_like(yt_ref)

    at = lax.dot_general(u_ref[...], h2_ref[...], NT, preferred_element_type=f32)

    for t in range(n_tiles):
        for p in range(n_parts):
            ws = [None] * per
            for h in range(PEER_HEADS):
                rk = rank_ref[h, t]
                f2 = f2_ref[h, t]
                for b in range(per):
                    i1 = j * n_blk + p * per + b
                    cn = cnt_ref[h, t, pl.ds(i1, 1), :].astype(bf16)
                    ee = e1_ref[h, t, pl.ds(i1, 1), :].astype(bf16)
                    term = jnp.where(rk < cn, f2, 0) * ee
                    ws[b] = term if ws[b] is None else ws[b] + term
            for b in range(per):
                r0 = (p * per + b) * PEER_NKEYS
                wg_ref[r0:r0 + PEER_NKEYS, t * LANE:(t + 1) * LANE] = ws[b]

    y_step = None
    for p in range(n_parts):
        rows = slice(p * per * PEER_NKEYS, (p + 1) * per * PEER_NKEYS)
        a = at[rows, :]
        act = (a * (1.0 + lax.erf(a * (2.0 ** -0.5)))).astype(bf16)
        wg = wg_ref[rows, :] * act
        part = jnp.dot(vt_ref[:, rows], wg, preferred_element_type=f32)
        y_step = part if y_step is None else y_step + part
    yt_ref[...] += y_step

    @pl.when(j == pl.num_programs(1) - 1)
    def _():
        y = yt_ref[...].T
        ms = jnp.mean(y * y, axis=-1, keepdims=True)
        o_ref[...] = lat_ref[...] + g2_ref[0] * (y * lax.rsqrt(ms + EPS) * nw_ref[...])


def _peer(h2, lat, wq, keys, u, vt, g2, nw, *, tokens_per_batch, tt, ec):
    N, D = h2.shape
    E = u.shape[0]
    tiles_per_batch = tokens_per_batch // tt
    sel_width = 2 if (tt // LANE) % 2 == 0 else 1
    sel = lambda dt: pltpu.VMEM((PEER_HEADS, tt // LANE, PEER_NKEYS, LANE), dt)
    return pl.pallas_call(
        _peer_kernel,
        name="peer",
        out_shape=jax.ShapeDtypeStruct((N, D), f32),
        grid=(N // tt, E // ec),
        in_specs=[pl.BlockSpec((tt, D), lambda i, j: (i, 0)),
                  pl.BlockSpec((tt, D), lambda i, j: (i, 0)),
                  pl.BlockSpec(wq.shape, lambda i, j: (0, 0, 0)),
                  pl.BlockSpec(keys.shape, lambda i, j: (0, 0, 0, 0)),
                  pl.BlockSpec((ec, D), lambda i, j: (j, 0)),
                  pl.BlockSpec((D, ec), lambda i, j: (0, j)),
                  pl.BlockSpec((1, 1, D), lambda i, j: (i // tiles_per_batch, 0, 0)),
                  pl.BlockSpec((1, D), lambda i, j: (0, 0))],
        out_specs=pl.BlockSpec((tt, D), lambda i, j: (i, 0)),
        scratch_shapes=[sel(bf16), sel(bf16), sel(f32), sel(f32),
                        pltpu.VMEM((tt, PEER_DQ), f32),
                        pltpu.VMEM((sel_width, PEER_TOPK, LANE), f32),
                        pltpu.VMEM((sel_width, PEER_TOPK, LANE), f32),
                        pltpu.VMEM((sel_width, CHUNK, LANE), f32),
                        pltpu.VMEM((ec, tt), bf16),
                        pltpu.VMEM((D, tt), f32)],
        compiler_params=_params(("arbitrary", "arbitrary")),
    )(h2, lat, wq, keys, u, vt, g2, nw)


def _dir_rows(t, n_heads, width=CHUNK):
    B, L, _ = t.shape
    return t.reshape(B, L, N_DIR, n_heads).transpose(0, 2, 3, 1).reshape(B, N_DIR, n_heads, L // width, width)


def _to_colmajor(t, rows):
    B, _, C = t.shape
    return t.reshape(B, rows, GRID_W, C).transpose(0, 2, 1, 3).reshape(B, GRID_W * rows, C)


def _pick(n, prefs):
    for p in prefs:
        if n % p == 0:
            return p
    raise ValueError(f"no tile in {prefs} divides {n}")


def kernel(x, c, ctx, c_ctx, w_mod, b_mod, norm_w, w_in, ml_gate_bias, dn_a_log, dn_dt_bias, dn_conv_w,
           ml_norm_w, dn_norm_w, w_branch_a, w_branch_b, w_out, peer_wq, peer_keys, peer_u, peer_v):
    B, L, D = x.shape
    Lc = ctx.shape[1]
    assert w_mod.shape[0] == 1, "single-layer block"
    assert L % GRID_W == 0 and L % CHUNK == 0 and Lc % CHUNK == 0
    rows = L // GRID_W
    assert rows % 8 == 0 and CHUNK % rows == 0
    assert D == ML_HEADS * ML_DV == DN_HEADS * DN_DV

    mod = _modulation(jnp.concatenate([c, c_ctx[None, :]], axis=0), w_mod[0], b_mod[0])
    mod = mod.reshape(B + 1, N_MOD, 1, D)
    sh1, sc1, g1, sh2, sc2, g2 = (mod[:B, i] for i in range(N_MOD))
    sh1c, sc1c = mod[B:, 0], mod[B:, 1]

    w = w_in[0]
    sizes = (512, 512, 1024, 1024, 8, 8, 1024, 1024, 1024, 1024, 16, 16, 1024, 1024)
    offs = [0]
    for s in sizes:
        offs.append(offs[-1] + s)
    col = lambda i: w[:, offs[i]:offs[i + 1]]
    w_main = jnp.concatenate([col(0), col(1), col(2), col(3), col(9), col(12), col(13)], axis=1).astype(bf16)
    w_dn = jnp.concatenate([col(6), col(7), col(8)], axis=1).astype(bf16)
    w_gate = jnp.concatenate([col(4), col(5), col(10), col(11)], axis=1)
    n_gate = w_gate.shape[1]
    w_gate = jnp.pad(w_gate, ((0, 0), (0, LANE - n_gate))).astype(bf16)

    nw = norm_w[0]
    tt_lat = _pick(L, (1024, 512, 256, 128, 64))
    tt_ctx = _pick(Lc, (256, 128, 64))
    proj = lambda xs, sc_, sh_, tt: (
        _project(xs, nw[0:1], sc_, sh_, w_main, tn=1024, tt=tt, head_major=False, out_dtype=bf16),
        _project(xs, nw[0:1], sc_, sh_, w_dn, tn=1024, tt=tt, head_major=True, out_dtype=f32),
        _project(xs, nw[0:1], sc_, sh_, w_gate, tn=LANE, tt=tt, head_major=False, out_dtype=f32))
    p_lat, dn_lat, g_lat = proj(x, sc1, sh1, tt_lat)
    p_ctx, dn_ctx, g_ctx = proj(ctx, sc1c, sh1c, tt_ctx)

    nh = N_DIR * ML_HEADS
    nd = N_DIR * DN_HEADS
    gi, gf = _dir_rows(g_lat[..., 0:nh], ML_HEADS), _dir_rows(g_lat[..., nh:2 * nh], ML_HEADS)
    gic, gfc = _dir_rows(g_ctx[..., 0:nh], ML_HEADS), _dir_rows(g_ctx[..., nh:2 * nh], ML_HEADS)
    dn_width = CHUNK * math.gcd(math.gcd(L // CHUNK, Lc // CHUNK), 4)
    be = _dir_rows(_to_colmajor(g_lat[..., 2 * nh:2 * nh + nd], rows), DN_HEADS, dn_width)
    ar = _dir_rows(_to_colmajor(g_lat[..., 2 * nh + nd:2 * nh + 2 * nd], rows), DN_HEADS, dn_width)
    bec = _dir_rows(g_ctx[..., 2 * nh:2 * nh + nd], DN_HEADS, dn_width)
    arc = _dir_rows(g_ctx[..., 2 * nh + nd:2 * nh + 2 * nd], DN_HEADS, dn_width)
    bias = jnp.broadcast_to(ml_gate_bias[0].astype(f32)[:, :, :, None, None], (2, N_DIR, ML_HEADS, 1, CHUNK))
    alog = jnp.broadcast_to(dn_a_log[0].astype(f32)[:, :, None, None], (N_DIR, DN_HEADS, 1, dn_width))
    dtb = jnp.broadcast_to(dn_dt_bias[0].astype(f32)[:, :, None, None], (N_DIR, DN_HEADS, 1, dn_width))

    hm = _mlstm(p_lat, p_ctx, gi, gf, gic, gfc, bias)

    taps = dn_conv_w[0].astype(f32).reshape(DN_CONV, 3 * DN_HEADS, LANE).transpose(1, 0, 2)
    taps = jnp.pad(taps, ((0, 0), (0, 8 - DN_CONV), (0, 0)))
    qkv_lat = _dn_prep(dn_lat, jnp.tile(taps, (1, 1, GRID_W)), seq_len=rows, n_cols=GRID_W)
    qkv_ctx = _dn_prep(dn_ctx, taps, seq_len=Lc, n_cols=1)
    hd = _gdn(qkv_lat, qkv_ctx, be, ar, bec, arc, alog, dtb, rows=rows)

    lat1, h2 = _merge(x, hm, hd, p_lat, ml_norm_w[0].reshape(1, D).astype(f32),
                      jnp.tile(dn_norm_w[0].astype(f32), DN_HEADS).reshape(1, D),
                      w_branch_a[0].astype(bf16), w_branch_b[0].astype(bf16), w_out[0].astype(bf16),
                      nw[1:2], nw[2:3], g1, sc2, sh2, tt=_pick(L, (256, 128, 64)))

    wq = peer_wq[0].astype(bf16).reshape(D, PEER_HEADS, PEER_DQ).transpose(1, 0, 2)
    out = _peer(h2.reshape(B * L, D), lat1.reshape(B * L, D), wq, peer_keys[0].astype(bf16),
                peer_u[0].astype(bf16), _transposed_bf16(peer_v[0], te=1024), g2, nw[3:4],
                tokens_per_batch=L, tt=_pick(L, (512, 256)), ec=1024)
    return out.reshape(B, L, D)
```

```python
import functools
import math

import jax
import jax.numpy as jnp
from jax import lax
from jax.experimental import pallas as pl
from jax.experimental.pallas import tpu as pltpu

f32 = jnp.float32
bf16 = jnp.bfloat16

EPS = 1e-6
GRID_W = 64
N_DIR = 2
N_MOD = 6
ML_HEADS, ML_DQK, ML_DV = 4, 128, 256
DN_HEADS, DN_DK, DN_DV, DN_CONV = 8, 128, 128, 5
CHUNK = 64
PEER_HEADS, PEER_NKEYS, PEER_DQ, PEER_TOPK = 8, 128, 256, 16
LANE = 128
VMEM_LIMIT = 56 * 1024 * 1024

NT = (((1,), (1,)), ((), ()))
TN = (((0,), (0,)), ((), ()))
NEG_INF = float("-inf")


def _params(sem, flags=None):
    return pltpu.CompilerParams(dimension_semantics=sem, vmem_limit_bytes=VMEM_LIMIT, flags=flags)


def _mod_kernel(c_ref, w_ref, b_ref, o_ref):
    c = c_ref[...]
    s = c * jax.nn.sigmoid(c)
    o_ref[...] = jnp.dot(s, w_ref[...], preferred_element_type=f32) + b_ref[...]


def _modulation(cond, w_mod, b_mod):
    n, d = cond.shape
    return pl.pallas_call(
        _mod_kernel,
        name="modulation",
        out_shape=jax.ShapeDtypeStruct((n, N_MOD * d), f32),
        grid=(N_MOD,),
        in_specs=[pl.BlockSpec((n, d), lambda j: (0, 0)),
                  pl.BlockSpec((d, d), lambda j: (0, j)),
                  pl.BlockSpec((1, d), lambda j: (0, j))],
        out_specs=pl.BlockSpec((n, d), lambda j: (0, j)),
        compiler_params=_params(("arbitrary",)),
    )(cond, w_mod, b_mod.reshape(1, -1))


def _proj_kernel(x_ref, nw_ref, sc_ref, sh_ref, w_ref, o_ref, h_ref, *, head_major):
    @pl.when(pl.program_id(2) == 0)
    def _():
        x = x_ref[0]
        ms = jnp.mean(x * x, axis=-1, keepdims=True)
        h = (x * lax.rsqrt(ms + EPS)) * nw_ref[...]
        h_ref[...] = (h * (1.0 + sc_ref[0]) + sh_ref[0]).astype(bf16)

    acc = jnp.dot(h_ref[...], w_ref[...], preferred_element_type=f32).astype(o_ref.dtype)
    if head_major:
        for i in range(acc.shape[1] // LANE):
            o_ref[0, i] = acc[:, i * LANE:(i + 1) * LANE]
    else:
        o_ref[0] = acc


def _project(xs, nw, sc, sh, w, *, tn, tt, head_major, out_dtype):
    B, L, D = xs.shape
    N = w.shape[1]
    per_batch = sc.shape[0] == B and B > 1
    mod_map = (lambda b, t, j: (b, 0, 0)) if per_batch else (lambda b, t, j: (0, 0, 0))
    if head_major:
        out_shape = jax.ShapeDtypeStruct((B, N // LANE, L, LANE), out_dtype)
        out_spec = pl.BlockSpec((1, tn // LANE, tt, LANE), lambda b, t, j: (b, j, t, 0))
    else:
        out_shape = jax.ShapeDtypeStruct((B, L, N), out_dtype)
        out_spec = pl.BlockSpec((1, tt, tn), lambda b, t, j: (b, t, j))
    return pl.pallas_call(
        functools.partial(_proj_kernel, head_major=head_major),
        name="in_proj",
        out_shape=out_shape,
        grid=(B, L // tt, N // tn),
        in_specs=[pl.BlockSpec((1, tt, D), lambda b, t, j: (b, t, 0)),
                  pl.BlockSpec((1, D), lambda b, t, j: (0, 0)),
                  pl.BlockSpec((1, 1, D), mod_map),
                  pl.BlockSpec((1, 1, D), mod_map),
                  pl.BlockSpec((D, tn), lambda b, t, j: (0, j))],
        out_specs=out_spec,
        scratch_shapes=[pltpu.VMEM((tt, D), bf16)],
        compiler_params=_params(("arbitrary", "arbitrary", "arbitrary")),
    )(xs, nw, sc, sh, w)


def _chunk_masks(reverse, n=1):
    size = n * CHUNK
    row = lax.broadcasted_iota(jnp.int32, (size, size), 0)
    col = lax.broadcasted_iota(jnp.int32, (size, size), 1)
    eye = row == col
    same = (row >> 6) == (col >> 6) if n > 1 else None
    both = (lambda m: jnp.logical_and(same, m)) if n > 1 else (lambda m: m)
    if reverse:
        return eye, both(col >= row), both(row >= col), same
    return eye, both(col <= row), both(row <= col), same


def _row_to_col(v_row, eye):
    return jnp.sum(jnp.where(eye, jnp.broadcast_to(v_row, eye.shape), 0.0), axis=1, keepdims=True)


def _cumsum_forms(v_row, eye, incl, incl_t):
    vb = jnp.broadcast_to(v_row, eye.shape)
    c_col = jnp.sum(jnp.where(incl, vb, 0.0), axis=1, keepdims=True)
    v_col = jnp.sum(jnp.where(eye, vb, 0.0), axis=1, keepdims=True)
    c_row = jnp.sum(jnp.where(incl_t, v_col, 0.0), axis=0, keepdims=True)
    return c_col, c_row


def _ml_chunks(problems, c_ref, n_ref, want_out):
    dirs = range(len(problems))
    gate = []
    for qs, k, v, ig_row, lf_row, m, masks in problems:
        eye, incl, incl_t, _ = masks
        b_col, b_row = _cumsum_forms(lf_row, eye, incl, incl_t)
        b_last = jnp.sum(lf_row, axis=1, keepdims=True)
        g_row = b_last - b_row + ig_row
        m_chunk = jnp.max(g_row, axis=1, keepdims=True)
        e_col = _row_to_col(jnp.exp(g_row - m_chunk), eye)
        ek = k * e_col
        gate.append((b_col, b_row, b_last, m_chunk, ek))
    v16 = [p[2].astype(bf16) for p in problems]
    kv = [lax.dot_general(gate[d][4].astype(bf16), v16[d], TN, preferred_element_type=f32) for d in dirs]
    c_old = [c_ref[d] for d in dirs]
    n_old = [n_ref[d] for d in dirs]
    hs = [None for _ in dirs]
    if want_out:
        q16 = [p[0].astype(bf16) for p in problems]
        qk = [lax.dot_general(q16[d], problems[d][1].astype(bf16), NT, preferred_element_type=f32) for d in dirs]
        qc = [jnp.dot(q16[d], c_old[d].astype(bf16), preferred_element_type=f32) for d in dirs]
        w_in, m_in = [], []
        for d in dirs:
            b_col, b_row = gate[d][0], gate[d][1]
            dlog = jnp.where(problems[d][6][1], b_col - b_row + problems[d][3], NEG_INF)
            mi = jnp.max(dlog, axis=1, keepdims=True)
            m_in.append(mi)
            w_in.append(jnp.exp(dlog - mi) * qk[d])
        num_in = [jnp.dot(w_in[d].astype(bf16), v16[d], preferred_element_type=f32) for d in dirs]
        for d in dirs:
            qs, m = problems[d][0], problems[d][5]
            den_in = jnp.sum(w_in[d], axis=1, keepdims=True)
            m_inter = gate[d][0] + m
            m_t = jnp.maximum(m_inter, m_in[d])
            a = jnp.exp(m_inter - m_t)
            r = jnp.exp(m_in[d] - m_t)
            num = a * qc[d] + r * num_in[d]
            den = a * jnp.sum(qs * n_old[d], axis=1, keepdims=True) + r * den_in
            hs[d] = num / jnp.maximum(jnp.abs(den), jnp.exp(-m_t))
    ms = []
    for d in dirs:
        _, _, b_last, m_chunk, ek = gate[d]
        m = problems[d][5]
        m_new = jnp.maximum(b_last + m, m_chunk)
        sp = jnp.exp(b_last + m - m_new)
        sc = jnp.exp(m_chunk - m_new)
        c_ref[d] = sp * c_old[d] + sc * kv[d]
        n_ref[d] = sp * n_old[d] + sc * jnp.sum(ek, axis=0, keepdims=True)
        ms.append(m_new)
    return hs, ms


def _mlstm_kernel(q_ref, k_ref, v_ref, kc_ref, vc_ref, gi_ref, gf_ref, gic_ref, gfc_ref, bias_ref,
                  o_ref, c_ref, n_ref):
    n_lat = q_ref.shape[1] // CHUNK
    n_ctx = kc_ref.shape[1] // CHUNK
    scale = ML_DQK ** -0.5
    c_ref[...] = jnp.zeros_like(c_ref)
    n_ref[...] = jnp.zeros_like(n_ref)
    o_ref[...] = jnp.zeros_like(o_ref)
    masks = [_chunk_masks(False), _chunk_masks(True)]

    n_heads = q_ref.shape[2] // ML_DQK
    chains = [(hh, d) for hh in range(n_heads) for d in range(N_DIR)]
    qk_cols = lambda hh: slice(hh * ML_DQK, (hh + 1) * ML_DQK)
    v_cols = lambda hh: slice(hh * ML_DV, (hh + 1) * ML_DV)

    def gates(i_ref, f_ref, hh, d, c):
        ig = i_ref[0, d, hh, pl.ds(c, 1), :] + bias_ref[0, d, hh]
        lf = jax.nn.log_sigmoid(f_ref[0, d, hh, pl.ds(c, 1), :] + bias_ref[1, d, hh])
        return ig, lf

    def ctx_body(i, ms):
        problems = []
        for n, (hh, d) in enumerate(chains):
            c = i if d == 0 else n_ctx - 1 - i
            sl = pl.ds(pl.multiple_of(c * CHUNK, CHUNK), CHUNK)
            ig, lf = gates(gic_ref, gfc_ref, hh, d, c)
            problems.append((None, kc_ref[0, sl, qk_cols(hh)].astype(f32), vc_ref[0, sl, v_cols(hh)], ig, lf,
                             ms[n], masks[d]))
        _, out = _ml_chunks(problems, c_ref, n_ref, False)
        return tuple(out)

    def lat_body(i, ms):
        problems, slices = [], []
        for n, (hh, d) in enumerate(chains):
            c = i if d == 0 else n_lat - 1 - i
            sl = pl.ds(pl.multiple_of(c * CHUNK, CHUNK), CHUNK)
            ig, lf = gates(gi_ref, gf_ref, hh, d, c)
            problems.append((q_ref[0, sl, qk_cols(hh)].astype(f32) * scale, k_ref[0, sl, qk_cols(hh)].astype(f32),
                             v_ref[0, sl, v_cols(hh)], ig, lf, ms[n], masks[d]))
            slices.append(sl)
        hs, out = _ml_chunks(problems, c_ref, n_ref, True)
        for n, (hh, d) in enumerate(chains):
            o_ref[0, slices[n], v_cols(hh)] += hs[n]
        return tuple(out)

    m0 = tuple(jnp.zeros((1, 1), f32) for _ in chains)
    ms = lax.fori_loop(0, n_ctx, ctx_body, m0)
    lax.fori_loop(0, n_lat, lat_body, ms)


def _mlstm(p_lat, p_ctx, gi, gf, gic, gfc, bias):
    B, L, _ = p_lat.shape
    Lc = p_ctx.shape[1]
    n_lat, n_ctx = L // CHUNK, Lc // CHUNK
    hp = 2
    qk_w, v_w = hp * ML_DQK, hp * ML_DV
    k_off = ML_HEADS * ML_DQK // qk_w
    v_off = 2 * ML_HEADS * ML_DQK // v_w
    gate_spec = lambda n: pl.BlockSpec((1, N_DIR, hp, n, CHUNK), lambda b, h: (b, 0, h, 0, 0))
    return pl.pallas_call(
        _mlstm_kernel,
        name="mlstm_scan",
        out_shape=jax.ShapeDtypeStruct((B, L, ML_HEADS * ML_DV), f32),
        grid=(B, ML_HEADS // hp),
        in_specs=[pl.BlockSpec((1, L, qk_w), lambda b, h: (b, 0, h)),
                  pl.BlockSpec((1, L, qk_w), lambda b, h: (b, 0, k_off + h)),
                  pl.BlockSpec((1, L, v_w), lambda b, h: (b, 0, v_off + h)),
                  pl.BlockSpec((1, Lc, qk_w), lambda b, h: (b, 0, k_off + h)),
                  pl.BlockSpec((1, Lc, v_w), lambda b, h: (b, 0, v_off + h)),
                  gate_spec(n_lat), gate_spec(n_lat), gate_spec(n_ctx), gate_spec(n_ctx),
                  pl.BlockSpec((2, N_DIR, hp, 1, CHUNK), lambda b, h: (0, 0, h, 0, 0))],
        out_specs=pl.BlockSpec((1, L, v_w), lambda b, h: (b, 0, h)),
        scratch_shapes=[pltpu.VMEM((hp * N_DIR, ML_DQK, ML_DV), f32), pltpu.VMEM((hp * N_DIR, 1, ML_DQK), f32)],
        compiler_params=_params(("arbitrary", "arbitrary")),
    )(p_lat, p_lat, p_lat, p_ctx, p_ctx, gi, gf, gic, gfc, bias)


def _dn_prep_kernel(x_ref, w_ref, o_ref, pad_ref, y_ref, *, seq_len, n_cols):
    kind = pl.program_id(1) // DN_HEADS
    half = DN_CONV // 2
    total = seq_len * n_cols
    pad = (pad_ref.shape[0] - total) // 2
    pad_ref[0:pad, :] = jnp.zeros((pad, LANE), f32)
    pad_ref[pad + total:2 * pad + total, :] = jnp.zeros((pad, LANE), f32)
    pad_ref[pad:pad + total, :] = x_ref[0, 0]
    y = jnp.zeros((total, LANE), f32)
    for t in range(DN_CONV):
        y = y + pad_ref[pl.ds(pad + (t - half) * n_cols, total), :] * w_ref[0, t:t + 1, :]
    y = y * jax.nn.sigmoid(y)
    q_scale = jnp.where(kind == 0, DN_DK ** -0.5, 1.0).astype(f32)
    ss = jnp.sum(y * y, axis=-1, keepdims=True)
    y = y * jnp.where(kind == 2, 1.0, lax.rsqrt(ss + EPS) * q_scale)
    if n_cols > 1:
        y = pltpu.einshape("rcd->crd", y.reshape(seq_len, n_cols, LANE))
    o_ref[0, 0] = y.reshape(o_ref.shape[2:])


def _dn_prep(raw, wt, *, seq_len, n_cols):
    B, G, L, _ = raw.shape
    n_chunks = L // CHUNK
    pad = -(-(DN_CONV // 2) * n_cols // 8) * 8
    return pl.pallas_call(
        functools.partial(_dn_prep_kernel, seq_len=seq_len, n_cols=n_cols),
        name="gdn_prep",
        out_shape=jax.ShapeDtypeStruct((B, G, n_chunks, CHUNK, LANE), f32),
        grid=(B, G),
        in_specs=[pl.BlockSpec((1, 1, L, LANE), lambda b, g: (b, g, 0, 0)),
                  pl.BlockSpec((1, 8, LANE), lambda b, g: (g, 0, 0))],
        out_specs=pl.BlockSpec((1, 1, n_chunks, CHUNK, LANE), lambda b, g: (b, g, 0, 0, 0)),
        scratch_shapes=[pltpu.VMEM((L + 2 * pad, LANE), f32), pltpu.VMEM((L, LANE), f32)],
        compiler_params=_params(("arbitrary", "arbitrary")),
    )(raw, wt)


def _mm16(a, b):
    return jnp.dot(a.astype(bf16), b.astype(bf16), preferred_element_type=f32)


def _unit_tri_inverses(mats, eye):
    row = lax.broadcasted_iota(jnp.int32, eye.shape, 0)
    col = lax.broadcasted_iota(jnp.int32, eye.shape, 1)
    same = lambda bits: (row >> bits) == (col >> bits)
    m8 = [jnp.where(same(3), m, 0.0) for m in mats]
    p2 = [_mm16(m, m) for m in m8]
    p4 = [_mm16(p, p) for p in p2]
    inv = [eye.astype(f32) - m for m in m8]
    inv = [i + _mm16(i, p) for i, p in zip(inv, p2)]
    inv = [i + _mm16(i, p) for i, p in zip(inv, p4)]
    for bits in (3, 4, 5):
        joining = jnp.logical_and(same(bits + 1), jnp.logical_not(same(bits)))
        right = [_mm16(jnp.where(joining, m, 0.0), i) for m, i in zip(mats, inv)]
        inv = [i - _mm16(i, r) for i, r in zip(inv, right)]
    return inv


def _dn_prepare(problems):
    n = len(problems)
    pre = []
    for q, k, v, braw_row, araw_row, a_scale, dt_bias, masks in problems:
        eye, incl, incl_t, same = masks
        beta_col = _row_to_col(jax.nn.sigmoid(braw_row), eye)
        g_row = a_scale * jax.nn.softplus(araw_row + dt_bias)
        gb = jnp.broadcast_to(g_row, eye.shape)
        gc_col = jnp.sum(jnp.where(incl, gb, 0.0), axis=1, keepdims=True)
        g_col = jnp.sum(jnp.where(eye, gb, 0.0), axis=1, keepdims=True)
        gc_row = jnp.sum(jnp.where(incl_t, g_col, 0.0), axis=0, keepdims=True)
        if same is None:
            g_last = jnp.sum(g_row, axis=1, keepdims=True)
        else:
            g_last = jnp.sum(jnp.where(same, gb, 0.0), axis=1, keepdims=True)
        gam = jnp.exp(jnp.where(incl, gc_col - gc_row, NEG_INF))
        pre.append((beta_col, gc_col, g_last, gam, k * beta_col, k.astype(bf16)))
    eye = problems[0][7][0]
    gram = [lax.dot_general(pre[i][4].astype(bf16), pre[i][5], NT, preferred_element_type=f32) for i in range(n)]
    mats = []
    for i in range(n):
        _, incl, _, _ = problems[i][7]
        mats.append(jnp.where(jnp.logical_and(incl, jnp.logical_not(eye)), gram[i] * pre[i][3], 0.0))
    inv = [m.astype(bf16) for m in _unit_tri_inverses(mats, eye)]
    egc = [jnp.exp(pre[i][1]) for i in range(n)]
    u = [jnp.dot(inv[i], (problems[i][2] * pre[i][0]).astype(bf16), preferred_element_type=f32) for i in range(n)]
    w = [jnp.dot(inv[i], (pre[i][4] * egc[i]).astype(bf16), preferred_element_type=f32) for i in range(n)]
    a_qk = [None if problems[i][0] is None else
            lax.dot_general(problems[i][0].astype(bf16), pre[i][5], NT, preferred_element_type=f32) * pre[i][3]
            for i in range(n)]
    out = []
    for i in range(n):
        q, k = problems[i][0], problems[i][1]
        res = [u[i], w[i].astype(bf16), k * jnp.exp(pre[i][2] - pre[i][1]), jnp.exp(pre[i][2])]
        if q is not None:
            res += [(q * egc[i]).astype(bf16), a_qk[i].astype(bf16)]
        out.append(res)
    return out


def _gdn_kernel(q_ref, k_ref, v_ref, kc_ref, vc_ref, be_ref, ar_ref, bec_ref, arc_ref, alog_ref, dtb_ref,
                o_ref, s_ref, acc_ref, u_ref, w_ref, kd_ref, dl_ref, qg_ref, aq_ref, *, rows, group):
    n_lat = q_ref.shape[2]
    n_ctx = kc_ref.shape[2]
    size = group * CHUNK
    masks = [_chunk_masks(False, group), _chunk_masks(True, group)]

    n_heads = k_ref.shape[1]
    chains = [(hh, d) for hh in range(n_heads) for d in range(N_DIR)]

    def prepare(i, base, qr, kr, vr, ber, arr):
        problems = []
        blk = pl.ds(i * group, group)
        for hh, d in chains:
            q = None if qr is None else qr[0, hh, blk].reshape(size, DN_DK)
            k = kr[0, hh, blk].reshape(size, DN_DK)
            v = vr[0, hh, blk].reshape(size, DN_DV)
            problems.append((q, k, v, ber[0, d, hh, pl.ds(i, 1), :], arr[0, d, hh, pl.ds(i, 1), :],
                             -jnp.exp(alog_ref[d, hh]), dtb_ref[d, hh], masks[d]))
        results = _dn_prepare(problems)
        for n, res in enumerate(results):
            dst = pl.ds(base + i * group, group)
            u_ref[n, dst] = res[0].reshape(group, CHUNK, DN_DV)
            w_ref[n, dst] = res[1].reshape(group, CHUNK, DN_DK)
            dl = jnp.broadcast_to(res[3], (size, LANE))
            for g in range(group):
                c = base + i * group + g
                kd_ref[n, c] = res[2][g * CHUNK:(g + 1) * CHUNK].T.astype(bf16)
                dl_ref[n, c] = dl[g * CHUNK:g * CHUNK + 8]
            if qr is not None:
                qg_ref[n, pl.ds(i * group, group)] = res[4].reshape(group, CHUNK, DN_DK)
                for g in range(group):
                    aq_ref[n, i * group + g] = res[5][g * CHUNK:(g + 1) * CHUNK, g * CHUNK:(g + 1) * CHUNK]

    def ctx_prep(i, carry):
        prepare(i, 0, None, kc_ref, vc_ref, bec_ref, arc_ref)
        return carry

    def lat_prep(i, carry):
        prepare(i, n_ctx, q_ref, k_ref, v_ref, be_ref, ar_ref)
        return carry

    lax.fori_loop(0, n_ctx // group, ctx_prep, 0)
    lax.fori_loop(0, n_lat // group, lat_prep, 0)

    s_ref[...] = jnp.zeros_like(s_ref)
    acc_ref[...] = jnp.zeros_like(acc_ref)

    def step(cs, cls):
        idx = range(len(chains))
        s_old = [s_ref[n] for n in idx]
        s16 = [s.astype(bf16) for s in s_old]
        ws = [jnp.dot(w_ref[n, cs[chains[n][1]]], s16[n], preferred_element_type=f32) for n in idx]
        vn16 = [(u_ref[n, cs[chains[n][1]]] - ws[n]).astype(bf16) for n in idx]
        upd = [jnp.dot(kd_ref[n, cs[chains[n][1]]], vn16[n], preferred_element_type=f32) for n in idx]
        for n in idx:
            s_ref[n] = dl_ref[n, cs[chains[n][1]]][0:1, :] * s_old[n] + upd[n]
        if cls is not None:
            inter = [jnp.dot(qg_ref[n, cls[chains[n][1]]], s16[n], preferred_element_type=f32) for n in idx]
            intra = [jnp.dot(aq_ref[n, cls[chains[n][1]]], vn16[n], preferred_element_type=f32) for n in idx]
            for n in idx:
                hh, d = chains[n]
                acc_ref[hh, cls[d]] += inter[n] + intra[n]

    def ctx_step(i, carry):
        step((i, n_ctx - 1 - i), None)
        return carry

    def lat_step(i, carry):
        step((n_ctx + i, n_ctx + n_lat - 1 - i), (i, n_lat - 1 - i))
        return carry

    lax.fori_loop(0, n_ctx, ctx_step, 0)
    lax.fori_loop(0, n_lat, lat_step, 0)
    for hh in range(n_heads):
        o = acc_ref[hh].reshape(GRID_W, rows, DN_DV)
        o_ref[0, hh] = pltpu.einshape("crd->rcd", o).reshape(rows * GRID_W, DN_DV)


def _gdn(qkv_lat, qkv_ctx, be, ar, bec, arc, alog, dtb, *, rows):
    B, _, n_lat, _, _ = qkv_lat.shape
    n_ctx = qkv_ctx.shape[2]
    L = n_lat * CHUNK
    H = DN_HEADS
    n_all = n_ctx + n_lat
    width = be.shape[-1]
    group = width // CHUNK
    hp = 2
    nc = hp * N_DIR
    blk = lambda n, off: pl.BlockSpec((1, hp, n, CHUNK, LANE), lambda b, h: (b, off // hp + h, 0, 0, 0))
    gate_spec = lambda n: pl.BlockSpec((1, N_DIR, hp, n // group, width), lambda b, h: (b, 0, h, 0, 0))
    const_spec = pl.BlockSpec((N_DIR, hp, 1, width), lambda b, h: (0, h, 0, 0))
    out = pl.pallas_call(
        functools.partial(_gdn_kernel, rows=rows, group=group),
        name="gdn_scan",
        out_shape=jax.ShapeDtypeStruct((B, H, L, LANE), f32),
        grid=(B, H // hp),
        in_specs=[blk(n_lat, 0), blk(n_lat, H), blk(n_lat, 2 * H), blk(n_ctx, H), blk(n_ctx, 2 * H),
                  gate_spec(n_lat), gate_spec(n_lat), gate_spec(n_ctx), gate_spec(n_ctx),
                  const_spec, const_spec],
        out_specs=pl.BlockSpec((1, hp, L, LANE), lambda b, h: (b, h, 0, 0)),
        scratch_shapes=[pltpu.VMEM((nc, DN_DK, DN_DV), f32), pltpu.VMEM((hp, n_lat, CHUNK, DN_DV), f32),
                        pltpu.VMEM((nc, n_all, CHUNK, DN_DV), f32),
                        pltpu.VMEM((nc, n_all, CHUNK, DN_DK), bf16),
                        pltpu.VMEM((nc, n_all, DN_DK, CHUNK), bf16),
                        pltpu.VMEM((nc, n_all, 8, LANE), f32),
                        pltpu.VMEM((nc, n_lat, CHUNK, DN_DK), bf16),
                        pltpu.VMEM((nc, n_lat, CHUNK, CHUNK), bf16)],
        compiler_params=_params(("arbitrary", "arbitrary")),
    )(qkv_lat, qkv_lat, qkv_lat, qkv_ctx, qkv_ctx, be, ar, bec, arc, alog, dtb)
    return out


def _merge_kernel(x_ref, hm_ref, hd_ref, o_ref_in, z_ref, ga_ref, gb_ref, mlw_ref, dnw_ref, wa_ref, wb_ref,
                  wo_ref, nw1_ref, nw2_ref, g1_ref, sc2_ref, sh2_ref, lat_ref, h2_ref):
    hm = hm_ref[0]
    parts = []
    for h in range(ML_HEADS):
        seg = hm[:, h * ML_DV:(h + 1) * ML_DV]
        ms = jnp.mean(seg * seg, axis=-1, keepdims=True)
        parts.append(seg * lax.rsqrt(ms + EPS))
    ym = jnp.concatenate(parts, axis=-1) * mlw_ref[...] * jax.nn.sigmoid(o_ref_in[0].astype(f32))
    parts = []
    for h in range(DN_HEADS):
        seg = hd_ref[0, h]
        ms = jnp.mean(seg * seg, axis=-1, keepdims=True)
        parts.append(seg * lax.rsqrt(ms + EPS))
    z = z_ref[0].astype(f32)
    yd = jnp.concatenate(parts, axis=-1) * dnw_ref[...] * (z * jax.nn.sigmoid(z))
    ya = jnp.dot(ym.astype(bf16), wa_ref[...], preferred_element_type=f32)
    yb = jnp.dot(yd.astype(bf16), wb_ref[...], preferred_element_type=f32)
    y = jax.nn.sigmoid(ga_ref[0].astype(f32)) * ya + jax.nn.sigmoid(gb_ref[0].astype(f32)) * yb
    ymix = jnp.dot(y.astype(bf16), wo_ref[...], preferred_element_type=f32)
    ms = jnp.mean(ymix * ymix, axis=-1, keepdims=True)
    lat = x_ref[0] + g1_ref[0] * (ymix * lax.rsqrt(ms + EPS) * nw1_ref[...])
    lat_ref[0] = lat
    ms = jnp.mean(lat * lat, axis=-1, keepdims=True)
    h2 = (lat * lax.rsqrt(ms + EPS) * nw2_ref[...]) * (1.0 + sc2_ref[0]) + sh2_ref[0]
    h2_ref[0] = h2.astype(bf16)


def _merge(x, hm, hd, p_lat, mlw, dnw, wa, wb, wo, nw1, nw2, g1, sc2, sh2, *, tt):
    B, L, D = x.shape
    tok = lambda off: pl.BlockSpec((1, tt, D), lambda b, t: (b, t, off))
    full = lambda a: pl.BlockSpec(a.shape, lambda b, t: (0,) * a.ndim)
    per_b = pl.BlockSpec((1, 1, D), lambda b, t: (b, 0, 0))
    return pl.pallas_call(
        _merge_kernel,
        name="merge",
        out_shape=(jax.ShapeDtypeStruct((B, L, D), f32), jax.ShapeDtypeStruct((B, L, D), bf16)),
        grid=(B, L // tt),
        in_specs=[tok(0), tok(0),
                  pl.BlockSpec((1, DN_HEADS, tt, LANE), lambda b, t: (b, 0, t, 0)),
                  tok(2), tok(3), tok(4), tok(5),
                  full(mlw), full(dnw), full(wa), full(wb), full(wo), full(nw1), full(nw2),
                  per_b, per_b, per_b],
        out_specs=(tok(0), tok(0)),
        compiler_params=_params(("arbitrary", "arbitrary")),
    )(x, hm, hd, p_lat, p_lat, p_lat, p_lat, mlw, dnw, wa, wb, wo, nw1, nw2, g1, sc2, sh2)


def _transpose_kernel(x_ref, o_ref):
    o_ref[...] = x_ref[...].T.astype(o_ref.dtype)


def _transposed_bf16(x, *, te):
    E, D = x.shape
    return pl.pallas_call(
        _transpose_kernel,
        name="table_transpose",
        out_shape=jax.ShapeDtypeStruct((D, E), bf16),
        grid=(E // te,),
        in_specs=[pl.BlockSpec((te, D), lambda i: (i, 0))],
        out_specs=pl.BlockSpec((D, te), lambda i: (0, i)),
        compiler_params=_params(("arbitrary",)),
    )(x)


def _cand_pairs():
    return [(i, j) for i in range(PEER_TOPK) for j in range(PEER_TOPK) if (i + 1) * (j + 1) <= PEER_TOPK]


def _peer_select(h2_ref, wq_ref, keys_ref, rank_ref, f2_ref, cnt_ref, e1_ref, qh_ref, a_ref, b_ref, cand_ref):
    n_tiles = h2_ref.shape[0] // LANE
    width = a_ref.shape[0]
    pairs = _cand_pairs()
    half = PEER_DQ // 2
    top = range(PEER_TOPK)

    def head_body(h, carry):
        qh_ref[...] = jnp.dot(h2_ref[...], wq_ref[h], preferred_element_type=f32)

        def tiles_body(g, carry2):
            lanes = range(width)
            tiles = [g * width + i for i in lanes]
            qt = [qh_ref[pl.ds(pl.multiple_of(t * LANE, LANE), LANE), :].astype(bf16) for t in tiles]
            s1 = [lax.dot_general(keys_ref[h, 0], q[:, :half], NT, preferred_element_type=f32) for q in qt]
            s2 = [lax.dot_general(keys_ref[h, 1], q[:, half:], NT, preferred_element_type=f32) for q in qt]
            w1, w2 = list(s1), list(s2)
            rank = [jnp.full(s.shape, float(PEER_TOPK), f32) for s in s2]
            for r in top:
                m1 = [jnp.max(w, axis=0, keepdims=True) for w in w1]
                m2 = [jnp.max(w, axis=0, keepdims=True) for w in w2]
                hit = [w2[i] == m2[i] for i in lanes]
                for i in lanes:
                    a_ref[i, r:r + 1, :] = m1[i]
                    b_ref[i, r:r + 1, :] = m2[i]
                rank = [jnp.where(hit[i], float(r), rank[i]) for i in lanes]
                if r + 1 < PEER_TOPK:
                    w1 = [jnp.where(w1[i] == m1[i], NEG_INF, w1[i]) for i in lanes]
                    w2 = [jnp.where(hit[i], NEG_INF, w2[i]) for i in lanes]
            cand_ref[...] = jnp.full(cand_ref.shape, NEG_INF, f32)
            for c, (i1, i2) in enumerate(pairs):
                for i in lanes:
                    cand_ref[i, c:c + 1, :] = a_ref[i, i1:i1 + 1, :] + b_ref[i, i2:i2 + 1, :]
            cand = [cand_ref[i] for i in lanes]
            work = list(cand)
            for r in top:
                tau = [jnp.max(w, axis=0, keepdims=True) for w in work]
                if r + 1 < PEER_TOPK:
                    work = [jnp.where(work[i] == tau[i], NEG_INF, work[i]) for i in lanes]
            a0 = [a_ref[i, 0:1, :] for i in lanes]
            b0 = [b_ref[i, 0:1, :] for i in lanes]
            z = [jnp.sum(jnp.where(cand[i] >= tau[i], jnp.exp(cand[i] - (a0[i] + b0[i])), 0.0), axis=0,
                         keepdims=True) for i in lanes]
            cnt = [jnp.zeros(s.shape, f32) for s in s1]
            for r in top:
                cnt = [cnt[i] + jnp.where(s1[i] + b_ref[i, r:r + 1, :] >= tau[i], 1.0, 0.0) for i in lanes]
            for i in lanes:
                rank_ref[h, tiles[i]] = rank[i].astype(bf16)
                f2_ref[h, tiles[i]] = (jnp.exp(s2[i] - b0[i]) * (0.5 / z[i])).astype(bf16)
                cnt_ref[h, tiles[i]] = cnt[i]
                e1_ref[h, tiles[i]] = jnp.exp(s1[i] - a0[i])
            return carry2

        return lax.fori_loop(0, n_tiles // width, tiles_body, carry)

    lax.fori_loop(0, PEER_HEADS, head_body, 0)


def _peer_kernel(h2_ref, lat_ref, wq_ref, keys_ref, u_ref, vt_ref, g2_ref, nw_ref, o_ref,
                 rank_ref, f2_ref, cnt_ref, e1_ref, qh_ref, a_ref, b_ref, cand_ref, wg_ref, yt_ref):
    j = pl.program_id(1)
    n_blk = u_ref.shape[0] // PEER_NKEYS
    n_tiles = h2_ref.shape[0] // LANE
    n_parts = 2
    per = n_blk // n_parts

    @pl.when(j == 0)
    def _():
        _peer_select(h2_ref, wq_ref, keys_ref, rank_ref, f2_ref, cnt_ref, e1_ref, qh_ref, a_ref, b_ref, cand_ref)
        yt_ref[...] = jnp.zeros_like(yt_ref)

    def gate_group(g, carry):
        t = g // n_parts
        p = g % n_parts
        ws = [None] * per
        for h in range(PEER_HEADS):
            rk = rank_ref[h, t]
            f2 = f2_ref[h, t]
            for b in range(per):
                i1 = j * n_blk + p * per + b
                cn = cnt_ref[h, t, pl.ds(i1, 1), :].astype(bf16)
                ee = e1_ref[h, t, pl.ds(i1, 1), :].astype(bf16)
                term = jnp.where(rk < cn, f2, 0) * ee
                ws[b] = term if ws[b] is None else ws[b] + term
        for b in range(per):
            r0 = pl.multiple_of((p * per + b) * PEER_NKEYS, PEER_NKEYS)
            wg_ref[t, pl.ds(r0, PEER_NKEYS), :] = ws[b]
        return carry

    lax.fori_loop(0, n_tiles * n_parts, gate_group, 0)

    at = lax.dot_general(u_ref[...], h2_ref[...], NT, preferred_element_type=f32)
    act = (at * (1.0 + lax.erf(at * (2.0 ** -0.5)))).astype(bf16)
    wg = jnp.concatenate([wg_ref[t] * act[:, t * LANE:(t + 1) * LANE] for t in range(n_tiles)], axis=1)
    yt_ref[...] += jnp.dot(vt_ref[...], wg, preferred_element_type=f32)

    @pl.when(j == pl.num_programs(1) - 1)
    def _():
        y = yt_ref[...].T
        ms = jnp.mean(y * y, axis=-1, keepdims=True)
        o_ref[...] = lat_ref[...] + g2_ref[0] * (y * lax.rsqrt(ms + EPS) * nw_ref[...])


def _peer(h2, lat, wq, keys, u, vt, g2, nw, *, tokens_per_batch, tt, ec):
    N, D = h2.shape
    E = u.shape[0]
    tiles_per_batch = tokens_per_batch // tt
    sel_width = 2 if (tt // LANE) % 2 == 0 else 1
    sel = lambda dt: pltpu.VMEM((PEER_HEADS, tt // LANE, PEER_NKEYS, LANE), dt)
    return pl.pallas_call(
        _peer_kernel,
        name="peer",
        out_shape=jax.ShapeDtypeStruct((N, D), f32),
        grid=(N // tt, E // ec),
        in_specs=[pl.BlockSpec((tt, D), lambda i, j: (i, 0)),
                  pl.BlockSpec((tt, D), lambda i, j: (i, 0)),
                  pl.BlockSpec(wq.shape, lambda i, j: (0, 0, 0)),
                  pl.BlockSpec(keys.shape, lambda i, j: (0, 0, 0, 0)),
                  pl.BlockSpec((ec, D), lambda i, j: (j, 0)),
                  pl.BlockSpec((D, ec), lambda i, j: (0, j)),
                  pl.BlockSpec((1, 1, D), lambda i, j: (i // tiles_per_batch, 0, 0)),
                  pl.BlockSpec((1, D), lambda i, j: (0, 0))],
        out_specs=pl.BlockSpec((tt, D), lambda i, j: (i, 0)),
        scratch_shapes=[sel(bf16), sel(bf16), sel(f32), sel(f32),
                        pltpu.VMEM((tt, PEER_DQ), f32),
                        pltpu.VMEM((sel_width, PEER_TOPK, LANE), f32),
                        pltpu.VMEM((sel_width, PEER_TOPK, LANE), f32),
                        pltpu.VMEM((sel_width, CHUNK, LANE), f32),
                        pltpu.VMEM((tt // LANE, ec, LANE), bf16),
                        pltpu.VMEM((D, tt), f32)],
        compiler_params=_params(("arbitrary", "arbitrary")),
    )(h2, lat, wq, keys, u, vt, g2, nw)


def _dir_rows(t, n_heads, width=CHUNK):
    B, L, _ = t.shape
    return t.reshape(B, L, N_DIR, n_heads).transpose(0, 2, 3, 1).reshape(B, N_DIR, n_heads, L // width, width)


def _to_colmajor(t, rows):
    B, _, C = t.shape
    return t.reshape(B, rows, GRID_W, C).transpose(0, 2, 1, 3).reshape(B, GRID_W * rows, C)


def _pick(n, prefs):
    for p in prefs:
        if n % p == 0:
            return p
    raise ValueError(f"no tile in {prefs} divides {n}")


def kernel(x, c, ctx, c_ctx, w_mod, b_mod, norm_w, w_in, ml_gate_bias, dn_a_log, dn_dt_bias, dn_conv_w,
           ml_norm_w, dn_norm_w, w_branch_a, w_branch_b, w_out, peer_wq, peer_keys, peer_u, peer_v):
    B, L, D = x.shape
    Lc = ctx.shape[1]
    assert w_mod.shape[0] == 1, "single-layer block"
    assert L % GRID_W == 0 and L % CHUNK == 0 and Lc % CHUNK == 0
    rows = L // GRID_W
    assert rows % 8 == 0 and CHUNK % rows == 0
    assert D == ML_HEADS * ML_DV == DN_HEADS * DN_DV

    mod = _modulation(jnp.concatenate([c, c_ctx[None, :]], axis=0), w_mod[0], b_mod[0])
    mod = mod.reshape(B + 1, N_MOD, 1, D)
    sh1, sc1, g1, sh2, sc2, g2 = (mod[:B, i] for i in range(N_MOD))
    sh1c, sc1c = mod[B:, 0], mod[B:, 1]

    w = w_in[0]
    sizes = (512, 512, 1024, 1024, 8, 8, 1024, 1024, 1024, 1024, 16, 16, 1024, 1024)
    offs = [0]
    for s in sizes:
        offs.append(offs[-1] + s)
    col = lambda i: w[:, offs[i]:offs[i + 1]]
    w_main = jnp.concatenate([col(0), col(1), col(2), col(3), col(9), col(12), col(13)], axis=1).astype(bf16)
    w_dn = jnp.concatenate([col(6), col(7), col(8)], axis=1).astype(bf16)
    w_gate = jnp.concatenate([col(4), col(5), col(10), col(11)], axis=1)
    n_gate = w_gate.shape[1]
    w_gate = jnp.pad(w_gate, ((0, 0), (0, LANE - n_gate))).astype(bf16)

    nw = norm_w[0]
    tt_lat = _pick(L, (1024, 512, 256, 128, 64))
    tt_ctx = _pick(Lc, (256, 128, 64))
    proj = lambda xs, sc_, sh_, tt: (
        _project(xs, nw[0:1], sc_, sh_, w_main, tn=1024, tt=tt, head_major=False, out_dtype=bf16),
        _project(xs, nw[0:1], sc_, sh_, w_dn, tn=1024, tt=tt, head_major=True, out_dtype=f32),
        _project(xs, nw[0:1], sc_, sh_, w_gate, tn=LANE, tt=tt, head_major=False, out_dtype=f32))
    p_lat, dn_lat, g_lat = proj(x, sc1, sh1, tt_lat)
    p_ctx, dn_ctx, g_ctx = proj(ctx, sc1c, sh1c, tt_ctx)

    nh = N_DIR * ML_HEADS
    nd = N_DIR * DN_HEADS
    gi, gf = _dir_rows(g_lat[..., 0:nh], ML_HEADS), _dir_rows(g_lat[..., nh:2 * nh], ML_HEADS)
    gic, gfc = _dir_rows(g_ctx[..., 0:nh], ML_HEADS), _dir_rows(g_ctx[..., nh:2 * nh], ML_HEADS)
    dn_width = CHUNK * math.gcd(math.gcd(L // CHUNK, Lc // CHUNK), 4)
    be = _dir_rows(_to_colmajor(g_lat[..., 2 * nh:2 * nh + nd], rows), DN_HEADS, dn_width)
    ar = _dir_rows(_to_colmajor(g_lat[..., 2 * nh + nd:2 * nh + 2 * nd], rows), DN_HEADS, dn_width)
    bec = _dir_rows(g_ctx[..., 2 * nh:2 * nh + nd], DN_HEADS, dn_width)
    arc = _dir_rows(g_ctx[..., 2 * nh + nd:2 * nh + 2 * nd], DN_HEADS, dn_width)
    bias = jnp.broadcast_to(ml_gate_bias[0].astype(f32)[:, :, :, None, None], (2, N_DIR, ML_HEADS, 1, CHUNK))
    alog = jnp.broadcast_to(dn_a_log[0].astype(f32)[:, :, None, None], (N_DIR, DN_HEADS, 1, dn_width))
    dtb = jnp.broadcast_to(dn_dt_bias[0].astype(f32)[:, :, None, None], (N_DIR, DN_HEADS, 1, dn_width))

    hm = _mlstm(p_lat, p_ctx, gi, gf, gic, gfc, bias)

    taps = dn_conv_w[0].astype(f32).reshape(DN_CONV, 3 * DN_HEADS, LANE).transpose(1, 0, 2)
    taps = jnp.pad(taps, ((0, 0), (0, 8 - DN_CONV), (0, 0)))
    qkv_lat = _dn_prep(dn_lat, taps, seq_len=rows, n_cols=GRID_W)
    qkv_ctx = _dn_prep(dn_ctx, taps, seq_len=Lc, n_cols=1)
    hd = _gdn(qkv_lat, qkv_ctx, be, ar, bec, arc, alog, dtb, rows=rows)

    lat1, h2 = _merge(x, hm, hd, p_lat, ml_norm_w[0].reshape(1, D).astype(f32),
                      jnp.tile(dn_norm_w[0].astype(f32), DN_HEADS).reshape(1, D),
                      w_branch_a[0].astype(bf16), w_branch_b[0].astype(bf16), w_out[0].astype(bf16),
                      nw[1:2], nw[2:3], g1, sc2, sh2, tt=_pick(L, (256, 128, 64)))

    wq = peer_wq[0].astype(bf16).reshape(D, PEER_HEADS, PEER_DQ).transpose(1, 0, 2)
    out = _peer(h2.reshape(B * L, D), lat1.reshape(B * L, D), wq, peer_keys[0].astype(bf16),
                peer_u[0].astype(bf16), _transposed_bf16(peer_v[0], te=1024), g2, nw[3:4],
                tokens_per_batch=L, tt=_pick(L, (512, 256)), ec=1024)
    return out.reshape(B, L, D)
```

```python
import functools
import math

import jax
import jax.numpy as jnp
from jax import lax
from jax.experimental import pallas as pl
from jax.experimental.pallas import tpu as pltpu

f32 = jnp.float32
bf16 = jnp.bfloat16

EPS = 1e-6
GRID_W = 64
N_DIR = 2
N_MOD = 6
ML_HEADS, ML_DQK, ML_DV = 4, 128, 256
DN_HEADS, DN_DK, DN_DV, DN_CONV = 8, 128, 128, 5
CHUNK = 64
PEER_HEADS, PEER_NKEYS, PEER_DQ, PEER_TOPK = 8, 128, 256, 16
LANE = 128
VMEM_LIMIT = 56 * 1024 * 1024

NT = (((1,), (1,)), ((), ()))
TN = (((0,), (0,)), ((), ()))
NEG_INF = float("-inf")


def _params(sem, flags=None):
    return pltpu.CompilerParams(dimension_semantics=sem, vmem_limit_bytes=VMEM_LIMIT, flags=flags)


def _mod_kernel(c_ref, w_ref, b_ref, o_ref):
    c = c_ref[...]
    s = c * jax.nn.sigmoid(c)
    o_ref[...] = jnp.dot(s, w_ref[...], preferred_element_type=f32) + b_ref[...]


def _modulation(cond, w_mod, b_mod):
    n, d = cond.shape
    return pl.pallas_call(
        _mod_kernel,
        name="modulation",
        out_shape=jax.ShapeDtypeStruct((n, N_MOD * d), f32),
        grid=(N_MOD,),
        in_specs=[pl.BlockSpec((n, d), lambda j: (0, 0)),
                  pl.BlockSpec((d, d), lambda j: (0, j)),
                  pl.BlockSpec((1, d), lambda j: (0, j))],
        out_specs=pl.BlockSpec((n, d), lambda j: (0, j)),
        compiler_params=_params(("arbitrary",)),
    )(cond, w_mod, b_mod.reshape(1, -1))


def _proj_kernel(x_ref, nw_ref, sc_ref, sh_ref, w_ref, o_ref, h_ref, *, head_major):
    @pl.when(pl.program_id(2) == 0)
    def _():
        x = x_ref[0]
        ms = jnp.mean(x * x, axis=-1, keepdims=True)
        h = (x * lax.rsqrt(ms + EPS)) * nw_ref[...]
        h_ref[...] = (h * (1.0 + sc_ref[0]) + sh_ref[0]).astype(bf16)

    acc = jnp.dot(h_ref[...], w_ref[...], preferred_element_type=f32).astype(o_ref.dtype)
    if head_major:
        for i in range(acc.shape[1] // LANE):
            o_ref[0, i] = acc[:, i * LANE:(i + 1) * LANE]
    else:
        o_ref[0] = acc


def _project(xs, nw, sc, sh, w, *, tn, tt, head_major, out_dtype):
    B, L, D = xs.shape
    N = w.shape[1]
    per_batch = sc.shape[0] == B and B > 1
    mod_map = (lambda b, t, j: (b, 0, 0)) if per_batch else (lambda b, t, j: (0, 0, 0))
    if head_major:
        out_shape = jax.ShapeDtypeStruct((B, N // LANE, L, LANE), out_dtype)
        out_spec = pl.BlockSpec((1, tn // LANE, tt, LANE), lambda b, t, j: (b, j, t, 0))
    else:
        out_shape = jax.ShapeDtypeStruct((B, L, N), out_dtype)
        out_spec = pl.BlockSpec((1, tt, tn), lambda b, t, j: (b, t, j))
    return pl.pallas_call(
        functools.partial(_proj_kernel, head_major=head_major),
        name="in_proj",
        out_shape=out_shape,
        grid=(B, L // tt, N // tn),
        in_specs=[pl.BlockSpec((1, tt, D), lambda b, t, j: (b, t, 0)),
                  pl.BlockSpec((1, D), lambda b, t, j: (0, 0)),
                  pl.BlockSpec((1, 1, D), mod_map),
                  pl.BlockSpec((1, 1, D), mod_map),
                  pl.BlockSpec((D, tn), lambda b, t, j: (0, j))],
        out_specs=out_spec,
        scratch_shapes=[pltpu.VMEM((tt, D), bf16)],
        compiler_params=_params(("arbitrary", "arbitrary", "arbitrary")),
    )(xs, nw, sc, sh, w)


def _chunk_masks(reverse, n=1):
    size = n * CHUNK
    row = lax.broadcasted_iota(jnp.int32, (size, size), 0)
    col = lax.broadcasted_iota(jnp.int32, (size, size), 1)
    eye = row == col
    same = (row >> 6) == (col >> 6) if n > 1 else None
    both = (lambda m: jnp.logical_and(same, m)) if n > 1 else (lambda m: m)
    if reverse:
        return eye, both(col >= row), both(row >= col), same
    return eye, both(col <= row), both(row <= col), same


def _row_to_col(v_row, eye):
    return jnp.sum(jnp.where(eye, jnp.broadcast_to(v_row, eye.shape), 0.0), axis=1, keepdims=True)


def _cumsum_forms(v_row, eye, incl, incl_t):
    vb = jnp.broadcast_to(v_row, eye.shape)
    c_col = jnp.sum(jnp.where(incl, vb, 0.0), axis=1, keepdims=True)
    v_col = jnp.sum(jnp.where(eye, vb, 0.0), axis=1, keepdims=True)
    c_row = jnp.sum(jnp.where(incl_t, v_col, 0.0), axis=0, keepdims=True)
    return c_col, c_row


def _ml_chunks(problems, c_ref, n_ref, want_out):
    dirs = range(len(problems))
    gate = []
    for qs, k, v, ig_row, lf_row, m, masks in problems:
        eye, incl, incl_t, _ = masks
        b_col, b_row = _cumsum_forms(lf_row, eye, incl, incl_t)
        b_last = jnp.sum(lf_row, axis=1, keepdims=True)
        g_row = b_last - b_row + ig_row
        m_chunk = jnp.max(g_row, axis=1, keepdims=True)
        e_col = _row_to_col(jnp.exp(g_row - m_chunk), eye)
        ek = k * e_col
        gate.append((b_col, b_row, b_last, m_chunk, ek))
    v16 = [p[2].astype(bf16) for p in problems]
    kv = [lax.dot_general(gate[d][4].astype(bf16), v16[d], TN, preferred_element_type=f32) for d in dirs]
    c_old = [c_ref[d] for d in dirs]
    n_old = [n_ref[d] for d in dirs]
    hs = [None for _ in dirs]
    if want_out:
        q16 = [p[0].astype(bf16) for p in problems]
        qk = [lax.dot_general(q16[d], problems[d][1].astype(bf16), NT, preferred_element_type=f32) for d in dirs]
        qc = [jnp.dot(q16[d], c_old[d].astype(bf16), preferred_element_type=f32) for d in dirs]
        w_in, m_in = [], []
        for d in dirs:
            b_col, b_row = gate[d][0], gate[d][1]
            dlog = jnp.where(problems[d][6][1], b_col - b_row + problems[d][3], NEG_INF)
            mi = jnp.max(dlog, axis=1, keepdims=True)
            m_in.append(mi)
            w_in.append(jnp.exp(dlog - mi) * qk[d])
        num_in = [jnp.dot(w_in[d].astype(bf16), v16[d], preferred_element_type=f32) for d in dirs]
        for d in dirs:
            qs, m = problems[d][0], problems[d][5]
            den_in = jnp.sum(w_in[d], axis=1, keepdims=True)
            m_inter = gate[d][0] + m
            m_t = jnp.maximum(m_inter, m_in[d])
            a = jnp.exp(m_inter - m_t)
            r = jnp.exp(m_in[d] - m_t)
            num = a * qc[d] + r * num_in[d]
            den = a * jnp.sum(qs * n_old[d], axis=1, keepdims=True) + r * den_in
            hs[d] = num / jnp.maximum(jnp.abs(den), jnp.exp(-m_t))
    ms = []
    for d in dirs:
        _, _, b_last, m_chunk, ek = gate[d]
        m = problems[d][5]
        m_new = jnp.maximum(b_last + m, m_chunk)
        sp = jnp.exp(b_last + m - m_new)
        sc = jnp.exp(m_chunk - m_new)
        c_ref[d] = sp * c_old[d] + sc * kv[d]
        n_ref[d] = sp * n_old[d] + sc * jnp.sum(ek, axis=0, keepdims=True)
        ms.append(m_new)
    return hs, ms


def _mlstm_kernel(q_ref, k_ref, v_ref, kc_ref, vc_ref, gi_ref, gf_ref, gic_ref, gfc_ref, bias_ref,
                  o_ref, c_ref, n_ref):
    n_lat = q_ref.shape[1] // CHUNK
    n_ctx = kc_ref.shape[1] // CHUNK
    scale = ML_DQK ** -0.5
    c_ref[...] = jnp.zeros_like(c_ref)
    n_ref[...] = jnp.zeros_like(n_ref)
    o_ref[...] = jnp.zeros_like(o_ref)
    masks = [_chunk_masks(False), _chunk_masks(True)]

    n_heads = q_ref.shape[2] // ML_DQK
    chains = [(hh, d) for hh in range(n_heads) for d in range(N_DIR)]
    qk_cols = lambda hh: slice(hh * ML_DQK, (hh + 1) * ML_DQK)
    v_cols = lambda hh: slice(hh * ML_DV, (hh + 1) * ML_DV)

    def gates(i_ref, f_ref, hh, d, c):
        ig = i_ref[0, d, hh, pl.ds(c, 1), :] + bias_ref[0, d, hh]
        lf = jax.nn.log_sigmoid(f_ref[0, d, hh, pl.ds(c, 1), :] + bias_ref[1, d, hh])
        return ig, lf

    def ctx_body(i, ms):
        problems = []
        for n, (hh, d) in enumerate(chains):
            c = i if d == 0 else n_ctx - 1 - i
            sl = pl.ds(pl.multiple_of(c * CHUNK, CHUNK), CHUNK)
            ig, lf = gates(gic_ref, gfc_ref, hh, d, c)
            problems.append((None, kc_ref[0, sl, qk_cols(hh)].astype(f32), vc_ref[0, sl, v_cols(hh)], ig, lf,
                             ms[n], masks[d]))
        _, out = _ml_chunks(problems, c_ref, n_ref, False)
        return tuple(out)

    def lat_body(i, ms):
        problems, slices = [], []
        for n, (hh, d) in enumerate(chains):
            c = i if d == 0 else n_lat - 1 - i
            sl = pl.ds(pl.multiple_of(c * CHUNK, CHUNK), CHUNK)
            ig, lf = gates(gi_ref, gf_ref, hh, d, c)
            problems.append((q_ref[0, sl, qk_cols(hh)].astype(f32) * scale, k_ref[0, sl, qk_cols(hh)].astype(f32),
                             v_ref[0, sl, v_cols(hh)], ig, lf, ms[n], masks[d]))
            slices.append(sl)
        hs, out = _ml_chunks(problems, c_ref, n_ref, True)
        for n, (hh, d) in enumerate(chains):
            o_ref[0, slices[n], v_cols(hh)] += hs[n]
        return tuple(out)

    m0 = tuple(jnp.zeros((1, 1), f32) for _ in chains)
    ms = lax.fori_loop(0, n_ctx, ctx_body, m0)
    lax.fori_loop(0, n_lat, lat_body, ms)


def _mlstm(p_lat, p_ctx, gi, gf, gic, gfc, bias):
    B, L, _ = p_lat.shape
    Lc = p_ctx.shape[1]
    n_lat, n_ctx = L // CHUNK, Lc // CHUNK
    hp = ML_HEADS
    qk_w, v_w = hp * ML_DQK, hp * ML_DV
    k_off = ML_HEADS * ML_DQK // qk_w
    v_off = 2 * ML_HEADS * ML_DQK // v_w
    gate_spec = lambda n: pl.BlockSpec((1, N_DIR, hp, n, CHUNK), lambda b, h: (b, 0, h, 0, 0))
    return pl.pallas_call(
        _mlstm_kernel,
        name="mlstm_scan",
        out_shape=jax.ShapeDtypeStruct((B, L, ML_HEADS * ML_DV), f32),
        grid=(B, ML_HEADS // hp),
        in_specs=[pl.BlockSpec((1, L, qk_w), lambda b, h: (b, 0, h)),
                  pl.BlockSpec((1, L, qk_w), lambda b, h: (b, 0, k_off + h)),
                  pl.BlockSpec((1, L, v_w), lambda b, h: (b, 0, v_off + h)),
                  pl.BlockSpec((1, Lc, qk_w), lambda b, h: (b, 0, k_off + h)),
                  pl.BlockSpec((1, Lc, v_w), lambda b, h: (b, 0, v_off + h)),
                  gate_spec(n_lat), gate_spec(n_lat), gate_spec(n_ctx), gate_spec(n_ctx),
                  pl.BlockSpec((2, N_DIR, hp, 1, CHUNK), lambda b, h: (0, 0, h, 0, 0))],
        out_specs=pl.BlockSpec((1, L, v_w), lambda b, h: (b, 0, h)),
        scratch_shapes=[pltpu.VMEM((hp * N_DIR, ML_DQK, ML_DV), f32), pltpu.VMEM((hp * N_DIR, 1, ML_DQK), f32)],
        compiler_params=_params(("arbitrary", "arbitrary")),
    )(p_lat, p_lat, p_lat, p_ctx, p_ctx, gi, gf, gic, gfc, bias)


def _dn_prep_kernel(x_ref, w_ref, o_ref, pad_ref, *, seq_len, n_cols):
    nh = x_ref.shape[1]
    kind = (pl.program_id(1) * nh) // DN_HEADS
    half = DN_CONV // 2
    total = seq_len * n_cols
    pad = (pad_ref.shape[0] - total) // 2
    pad_ref[0:pad, :] = jnp.zeros((pad, LANE), f32)
    pad_ref[pad + total:2 * pad + total, :] = jnp.zeros((pad, LANE), f32)
    q_scale = jnp.where(kind == 0, DN_DK ** -0.5, 1.0).astype(f32)
    for hh in range(nh):
        pad_ref[pad:pad + total, :] = x_ref[0, hh]
        y = jnp.zeros((total, LANE), f32)
        for t in range(DN_CONV):
            y = y + pad_ref[pl.ds(pad + (t - half) * n_cols, total), :] * w_ref[hh, t:t + 1, :]
        y = y * jax.nn.sigmoid(y)
        ss = jnp.sum(y * y, axis=-1, keepdims=True)
        y = y * jnp.where(kind == 2, 1.0, lax.rsqrt(ss + EPS) * q_scale)
        if n_cols > 1:
            y = pltpu.einshape("rcd->crd", y.reshape(seq_len, n_cols, LANE))
        o_ref[0, hh] = y.reshape(o_ref.shape[2:])


def _dn_prep(raw, wt, *, seq_len, n_cols, heads_per_step):
    B, G, L, _ = raw.shape
    n_chunks = L // CHUNK
    nh = heads_per_step
    assert DN_HEADS % nh == 0
    pad = -(-(DN_CONV // 2) * n_cols // 8) * 8
    return pl.pallas_call(
        functools.partial(_dn_prep_kernel, seq_len=seq_len, n_cols=n_cols),
        name="gdn_prep",
        out_shape=jax.ShapeDtypeStruct((B, G, n_chunks, CHUNK, LANE), f32),
        grid=(B, G // nh),
        in_specs=[pl.BlockSpec((1, nh, L, LANE), lambda b, g: (b, g, 0, 0)),
                  pl.BlockSpec((nh, 8, LANE), lambda b, g: (g, 0, 0))],
        out_specs=pl.BlockSpec((1, nh, n_chunks, CHUNK, LANE), lambda b, g: (b, g, 0, 0, 0)),
        scratch_shapes=[pltpu.VMEM((L + 2 * pad, LANE), f32)],
        compiler_params=_params(("arbitrary", "arbitrary")),
    )(raw, wt)


def _mm16(a, b):
    return jnp.dot(a.astype(bf16), b.astype(bf16), preferred_element_type=f32)


def _unit_tri_inverses(mats, eye):
    row = lax.broadcasted_iota(jnp.int32, eye.shape, 0)
    col = lax.broadcasted_iota(jnp.int32, eye.shape, 1)
    same = lambda bits: (row >> bits) == (col >> bits)
    m8 = [jnp.where(same(3), m, 0.0) for m in mats]
    p2 = [_mm16(m, m) for m in m8]
    p4 = [_mm16(p, p) for p in p2]
    inv = [eye.astype(f32) - m for m in m8]
    inv = [i + _mm16(i, p) for i, p in zip(inv, p2)]
    inv = [i + _mm16(i, p) for i, p in zip(inv, p4)]
    for bits in (3, 4, 5):
        joining = jnp.logical_and(same(bits + 1), jnp.logical_not(same(bits)))
        right = [_mm16(jnp.where(joining, m, 0.0), i) for m, i in zip(mats, inv)]
        inv = [i - _mm16(i, r) for i, r in zip(inv, right)]
    return inv


def _dn_prepare(problems):
    n = len(problems)
    pre = []
    for q, k, v, braw_row, araw_row, a_scale, dt_bias, masks in problems:
        eye, incl, incl_t, same = masks
        beta_col = _row_to_col(jax.nn.sigmoid(braw_row), eye)
        g_row = a_scale * jax.nn.softplus(araw_row + dt_bias)
        gb = jnp.broadcast_to(g_row, eye.shape)
        gc_col = jnp.sum(jnp.where(incl, gb, 0.0), axis=1, keepdims=True)
        g_col = jnp.sum(jnp.where(eye, gb, 0.0), axis=1, keepdims=True)
        gc_row = jnp.sum(jnp.where(incl_t, g_col, 0.0), axis=0, keepdims=True)
        if same is None:
            g_last = jnp.sum(g_row, axis=1, keepdims=True)
        else:
            g_last = jnp.sum(jnp.where(same, gb, 0.0), axis=1, keepdims=True)
        gam = jnp.exp(jnp.where(incl, gc_col - gc_row, NEG_INF))
        pre.append((beta_col, gc_col, g_last, gam, k * beta_col, k.astype(bf16)))
    eye = problems[0][7][0]
    gram = [lax.dot_general(pre[i][4].astype(bf16), pre[i][5], NT, preferred_element_type=f32) for i in range(n)]
    mats = []
    for i in range(n):
        _, incl, _, _ = problems[i][7]
        mats.append(jnp.where(jnp.logical_and(incl, jnp.logical_not(eye)), gram[i] * pre[i][3], 0.0))
    inv = [m.astype(bf16) for m in _unit_tri_inverses(mats, eye)]
    egc = [jnp.exp(pre[i][1]) for i in range(n)]
    rhs = [jnp.concatenate([problems[i][2] * pre[i][0], pre[i][4] * egc[i]], axis=1).astype(bf16) for i in range(n)]
    uw = [jnp.dot(inv[i], rhs[i], preferred_element_type=f32) for i in range(n)]
    u = [x[:, :DN_DV] for x in uw]
    w = [x[:, DN_DV:] for x in uw]
    a_qk = [None if problems[i][0] is None else
            lax.dot_general(problems[i][0].astype(bf16), pre[i][5], NT, preferred_element_type=f32) * pre[i][3]
            for i in range(n)]
    out = []
    for i in range(n):
        q, k = problems[i][0], problems[i][1]
        res = [u[i], w[i].astype(bf16), k * jnp.exp(pre[i][2] - pre[i][1]), jnp.exp(pre[i][2])]
        if q is not None:
            res += [(q * egc[i]).astype(bf16), a_qk[i].astype(bf16)]
        out.append(res)
    return out


def _gdn_kernel(q_ref, k_ref, v_ref, kc_ref, vc_ref, be_ref, ar_ref, bec_ref, arc_ref, alog_ref, dtb_ref,
                o_ref, s_ref, acc_ref, u_ref, w_ref, kd_ref, dl_ref, qg_ref, aq_ref, *, rows, group):
    n_lat = q_ref.shape[2]
    n_ctx = kc_ref.shape[2]
    size = group * CHUNK
    masks = [_chunk_masks(False, group), _chunk_masks(True, group)]

    n_heads = k_ref.shape[1]
    chains = [(hh, d) for hh in range(n_heads) for d in range(N_DIR)]

    def prepare(i, base, qr, kr, vr, ber, arr):
        problems = []
        blk = pl.ds(i * group, group)
        for hh, d in chains:
            q = None if qr is None else qr[0, hh, blk].reshape(size, DN_DK)
            k = kr[0, hh, blk].reshape(size, DN_DK)
            v = vr[0, hh, blk].reshape(size, DN_DV)
            problems.append((q, k, v, ber[0, d, hh, pl.ds(i, 1), :], arr[0, d, hh, pl.ds(i, 1), :],
                             -jnp.exp(alog_ref[d, hh]), dtb_ref[d, hh], masks[d]))
        results = _dn_prepare(problems)
        for n, res in enumerate(results):
            dst = pl.ds(base + i * group, group)
            u_ref[n, dst] = res[0].reshape(group, CHUNK, DN_DV)
            w_ref[n, dst] = res[1].reshape(group, CHUNK, DN_DK)
            dl = jnp.broadcast_to(res[3], (size, LANE))
            for g in range(group):
                c = base + i * group + g
                kd_ref[n, c] = res[2][g * CHUNK:(g + 1) * CHUNK].T.astype(bf16)
                dl_ref[n, c] = dl[g * CHUNK:g * CHUNK + 8]
            if qr is not None:
                qg_ref[n, pl.ds(i * group, group)] = res[4].reshape(group, CHUNK, DN_DK)
                for g in range(group):
                    aq_ref[n, i * group + g] = res[5][g * CHUNK:(g + 1) * CHUNK, g * CHUNK:(g + 1) * CHUNK]

    def ctx_prep(i, carry):
        prepare(i, 0, None, kc_ref, vc_ref, bec_ref, arc_ref)
        return carry

    def lat_prep(i, carry):
        prepare(i, n_ctx, q_ref, k_ref, v_ref, be_ref, ar_ref)
        return carry

    lax.fori_loop(0, n_ctx // group, ctx_prep, 0)
    lax.fori_loop(0, n_lat // group, lat_prep, 0)

    s_ref[...] = jnp.zeros_like(s_ref)
    acc_ref[...] = jnp.zeros_like(acc_ref)

    def step(cs, cls):
        idx = range(len(chains))
        s_old = [s_ref[n] for n in idx]
        s16 = [s.astype(bf16) for s in s_old]
        ws = [jnp.dot(w_ref[n, cs[chains[n][1]]], s16[n], preferred_element_type=f32) for n in idx]
        vn16 = [(u_ref[n, cs[chains[n][1]]] - ws[n]).astype(bf16) for n in idx]
        upd = [jnp.dot(kd_ref[n, cs[chains[n][1]]], vn16[n], preferred_element_type=f32) for n in idx]
        for n in idx:
            s_ref[n] = dl_ref[n, cs[chains[n][1]]][0:1, :] * s_old[n] + upd[n]
        if cls is not None:
            inter = [jnp.dot(qg_ref[n, cls[chains[n][1]]], s16[n], preferred_element_type=f32) for n in idx]
            intra = [jnp.dot(aq_ref[n, cls[chains[n][1]]], vn16[n], preferred_element_type=f32) for n in idx]
            for n in idx:
                hh, d = chains[n]
                acc_ref[hh, cls[d]] += inter[n] + intra[n]

    def ctx_step(i, carry):
        step((i, n_ctx - 1 - i), None)
        return carry

    def lat_step(i, carry):
        step((n_ctx + i, n_ctx + n_lat - 1 - i), (i, n_lat - 1 - i))
        return carry

    lax.fori_loop(0, n_ctx, ctx_step, 0)
    lax.fori_loop(0, n_lat, lat_step, 0)
    for hh in range(n_heads):
        o = acc_ref[hh].reshape(GRID_W, rows, DN_DV)
        o_ref[0, hh] = pltpu.einshape("crd->rcd", o).reshape(rows * GRID_W, DN_DV)


def _gdn(qkv_lat, qkv_ctx, be, ar, bec, arc, alog, dtb, *, rows):
    B, _, n_lat, _, _ = qkv_lat.shape
    n_ctx = qkv_ctx.shape[2]
    L = n_lat * CHUNK
    H = DN_HEADS
    n_all = n_ctx + n_lat
    width = be.shape[-1]
    group = width // CHUNK
    hp = 2
    nc = hp * N_DIR
    blk = lambda n, off: pl.BlockSpec((1, hp, n, CHUNK, LANE), lambda b, h: (b, off // hp + h, 0, 0, 0))
    gate_spec = lambda n: pl.BlockSpec((1, N_DIR, hp, n // group, width), lambda b, h: (b, 0, h, 0, 0))
    const_spec = pl.BlockSpec((N_DIR, hp, 1, width), lambda b, h: (0, h, 0, 0))
    out = pl.pallas_call(
        functools.partial(_gdn_kernel, rows=rows, group=group),
        name="gdn_scan",
        out_shape=jax.ShapeDtypeStruct((B, H, L, LANE), f32),
        grid=(B, H // hp),
        in_specs=[blk(n_lat, 0), blk(n_lat, H), blk(n_lat, 2 * H), blk(n_ctx, H), blk(n_ctx, 2 * H),
                  gate_spec(n_lat), gate_spec(n_lat), gate_spec(n_ctx), gate_spec(n_ctx),
                  const_spec, const_spec],
        out_specs=pl.BlockSpec((1, hp, L, LANE), lambda b, h: (b, h, 0, 0)),
        scratch_shapes=[pltpu.VMEM((nc, DN_DK, DN_DV), f32), pltpu.VMEM((hp, n_lat, CHUNK, DN_DV), f32),
                        pltpu.VMEM((nc, n_all, CHUNK, DN_DV), f32),
                        pltpu.VMEM((nc, n_all, CHUNK, DN_DK), bf16),
                        pltpu.VMEM((nc, n_all, DN_DK, CHUNK), bf16),
                        pltpu.VMEM((nc, n_all, 8, LANE), f32),
                        pltpu.VMEM((nc, n_lat, CHUNK, DN_DK), bf16),
                        pltpu.VMEM((nc, n_lat, CHUNK, CHUNK), bf16)],
        compiler_params=_params(("arbitrary", "arbitrary")),
    )(qkv_lat, qkv_lat, qkv_lat, qkv_ctx, qkv_ctx, be, ar, bec, arc, alog, dtb)
    return out


def _merge_kernel(x_ref, hm_ref, hd_ref, o_ref_in, z_ref, ga_ref, gb_ref, mlw_ref, dnw_ref, wa_ref, wb_ref,
                  wo_ref, nw1_ref, nw2_ref, g1_ref, sc2_ref, sh2_ref, lat_ref, h2_ref):
    hm = hm_ref[0]
    parts = []
    for h in range(ML_HEADS):
        seg = hm[:, h * ML_DV:(h + 1) * ML_DV]
        ms = jnp.mean(seg * seg, axis=-1, keepdims=True)
        parts.append(seg * lax.rsqrt(ms + EPS))
    ym = jnp.concatenate(parts, axis=-1) * mlw_ref[...] * jax.nn.sigmoid(o_ref_in[0].astype(f32))
    parts = []
    for h in range(DN_HEADS):
        seg = hd_ref[0, h]
        ms = jnp.mean(seg * seg, axis=-1, keepdims=True)
        parts.append(seg * lax.rsqrt(ms + EPS))
    z = z_ref[0].astype(f32)
    yd = jnp.concatenate(parts, axis=-1) * dnw_ref[...] * (z * jax.nn.sigmoid(z))
    ya = jnp.dot(ym.astype(bf16), wa_ref[...], preferred_element_type=f32)
    yb = jnp.dot(yd.astype(bf16), wb_ref[...], preferred_element_type=f32)
    y = jax.nn.sigmoid(ga_ref[0].astype(f32)) * ya + jax.nn.sigmoid(gb_ref[0].astype(f32)) * yb
    ymix = jnp.dot(y.astype(bf16), wo_ref[...], preferred_element_type=f32)
    ms = jnp.mean(ymix * ymix, axis=-1, keepdims=True)
    lat = x_ref[0] + g1_ref[0] * (ymix * lax.rsqrt(ms + EPS) * nw1_ref[...])
    lat_ref[0] = lat
    ms = jnp.mean(lat * lat, axis=-1, keepdims=True)
    h2 = (lat * lax.rsqrt(ms + EPS) * nw2_ref[...]) * (1.0 + sc2_ref[0]) + sh2_ref[0]
    h2_ref[0] = h2.astype(bf16)


def _merge(x, hm, hd, p_lat, mlw, dnw, wa, wb, wo, nw1, nw2, g1, sc2, sh2, *, tt):
    B, L, D = x.shape
    tok = lambda off: pl.BlockSpec((1, tt, D), lambda b, t: (b, t, off))
    full = lambda a: pl.BlockSpec(a.shape, lambda b, t: (0,) * a.ndim)
    per_b = pl.BlockSpec((1, 1, D), lambda b, t: (b, 0, 0))
    return pl.pallas_call(
        _merge_kernel,
        name="merge",
        out_shape=(jax.ShapeDtypeStruct((B, L, D), f32), jax.ShapeDtypeStruct((B, L, D), bf16)),
        grid=(B, L // tt),
        in_specs=[tok(0), tok(0),
                  pl.BlockSpec((1, DN_HEADS, tt, LANE), lambda b, t: (b, 0, t, 0)),
                  tok(2), tok(3), tok(4), tok(5),
                  full(mlw), full(dnw), full(wa), full(wb), full(wo), full(nw1), full(nw2),
                  per_b, per_b, per_b],
        out_specs=(tok(0), tok(0)),
        compiler_params=_params(("arbitrary", "arbitrary")),
    )(x, hm, hd, p_lat, p_lat, p_lat, p_lat, mlw, dnw, wa, wb, wo, nw1, nw2, g1, sc2, sh2)


def _transpose_kernel(x_ref, o_ref):
    o_ref[...] = x_ref[...].T.astype(o_ref.dtype)


def _transposed_bf16(x, *, te):
    E, D = x.shape
    return pl.pallas_call(
        _transpose_kernel,
        name="table_transpose",
        out_shape=jax.ShapeDtypeStruct((D, E), bf16),
        grid=(E // te,),
        in_specs=[pl.BlockSpec((te, D), lambda i: (i, 0))],
        out_specs=pl.BlockSpec((D, te), lambda i: (0, i)),
        compiler_params=_params(("arbitrary",)),
    )(x)


def _cand_pairs():
    return [(i, j) for i in range(PEER_TOPK) for j in range(PEER_TOPK) if (i + 1) * (j + 1) <= PEER_TOPK]


def _peer_select(h2_ref, wq_ref, keys_ref, rank_ref, f2_ref, cnt_ref, e1_ref, qh_ref, a_ref, b_ref, cand_ref):
    n_tiles = h2_ref.shape[0] // LANE
    width = a_ref.shape[0]
    pairs = _cand_pairs()
    half = PEER_DQ // 2
    top = range(PEER_TOPK)

    def head_body(h, carry):
        qh_ref[...] = jnp.dot(h2_ref[...], wq_ref[h], preferred_element_type=f32)

        def tiles_body(g, carry2):
            lanes = range(width)
            tiles = [g * width + i for i in lanes]
            qt = [qh_ref[pl.ds(pl.multiple_of(t * LANE, LANE), LANE), :].astype(bf16) for t in tiles]
            s1 = [lax.dot_general(keys_ref[h, 0], q[:, :half], NT, preferred_element_type=f32) for q in qt]
            s2 = [lax.dot_general(keys_ref[h, 1], q[:, half:], NT, preferred_element_type=f32) for q in qt]
            w1, w2 = list(s1), list(s2)
            rank = [jnp.full(s.shape, float(PEER_TOPK), f32) for s in s2]
            for r in top:
                m1 = [jnp.max(w, axis=0, keepdims=True) for w in w1]
                m2 = [jnp.max(w, axis=0, keepdims=True) for w in w2]
                hit = [w2[i] == m2[i] for i in lanes]
                for i in lanes:
                    a_ref[i, r:r + 1, :] = m1[i]
                    b_ref[i, r:r + 1, :] = m2[i]
                rank = [jnp.where(hit[i], float(r), rank[i]) for i in lanes]
                if r + 1 < PEER_TOPK:
                    w1 = [jnp.where(w1[i] == m1[i], NEG_INF, w1[i]) for i in lanes]
                    w2 = [jnp.where(hit[i], NEG_INF, w2[i]) for i in lanes]
            cand_ref[...] = jnp.full(cand_ref.shape, NEG_INF, f32)
            for c, (i1, i2) in enumerate(pairs):
                for i in lanes:
                    cand_ref[i, c:c + 1, :] = a_ref[i, i1:i1 + 1, :] + b_ref[i, i2:i2 + 1, :]
            cand = [cand_ref[i] for i in lanes]
            work = list(cand)
            for r in top:
                tau = [jnp.max(w, axis=0, keepdims=True) for w in work]
                if r + 1 < PEER_TOPK:
                    work = [jnp.where(work[i] == tau[i], NEG_INF, work[i]) for i in lanes]
            a0 = [a_ref[i, 0:1, :] for i in lanes]
            b0 = [b_ref[i, 0:1, :] for i in lanes]
            z = [jnp.sum(jnp.where(cand[i] >= tau[i], jnp.exp(cand[i] - (a0[i] + b0[i])), 0.0), axis=0,
                         keepdims=True) for i in lanes]
            cnt = [jnp.zeros(s.shape, f32) for s in s1]
            for r in top:
                cnt = [cnt[i] + jnp.where(s1[i] + b_ref[i, r:r + 1, :] >= tau[i], 1.0, 0.0) for i in lanes]
            for i in lanes:
                rank_ref[h, tiles[i]] = rank[i].astype(bf16)
                f2_ref[h, tiles[i]] = (jnp.exp(s2[i] - b0[i]) * (0.5 / z[i])).astype(bf16)
                cnt_ref[h, tiles[i]] = cnt[i]
                e1_ref[h, tiles[i]] = jnp.exp(s1[i] - a0[i])
            return carry2

        return lax.fori_loop(0, n_tiles // width, tiles_body, carry)

    lax.fori_loop(0, PEER_HEADS, head_body, 0)


def _peer_kernel(h2_ref, lat_ref, wq_ref, keys_ref, u_ref, vt_ref, g2_ref, nw_ref, o_ref,
                 rank_ref, f2_ref, cnt_ref, e1_ref, qh_ref, a_ref, b_ref, cand_ref, wg_ref, yt_ref):
    j = pl.program_id(1)
    n_blk = u_ref.shape[0] // PEER_NKEYS
    n_tiles = h2_ref.shape[0] // LANE
    n_parts = 2
    per = n_blk // n_parts

    @pl.when(j == 0)
    def _():
        _peer_select(h2_ref, wq_ref, keys_ref, rank_ref, f2_ref, cnt_ref, e1_ref, qh_ref, a_ref, b_ref, cand_ref)
        yt_ref[...] = jnp.zeros_like(yt_ref)

    def gate_group(g, carry):
        t = g // n_parts
        p = g % n_parts
        ws = [None] * per
        for h in range(PEER_HEADS):
            rk = rank_ref[h, t]
            f2 = f2_ref[h, t]
            for b in range(per):
                i1 = j * n_blk + p * per + b
                cn = cnt_ref[h, t, pl.ds(i1, 1), :].astype(bf16)
                ee = e1_ref[h, t, pl.ds(i1, 1), :].astype(bf16)
                term = jnp.where(rk < cn, f2, 0) * ee
                ws[b] = term if ws[b] is None else ws[b] + term
        for b in range(per):
            r0 = pl.multiple_of((p * per + b) * PEER_NKEYS, PEER_NKEYS)
            wg_ref[t, pl.ds(r0, PEER_NKEYS), :] = ws[b]
        return carry

    lax.fori_loop(0, n_tiles * n_parts, gate_group, 0)

    at = lax.dot_general(u_ref[...], h2_ref[...], NT, preferred_element_type=f32)
    act = (at * (1.0 + lax.erf(at * (2.0 ** -0.5)))).astype(bf16)
    wg = jnp.concatenate([wg_ref[t] * act[:, t * LANE:(t + 1) * LANE] for t in range(n_tiles)], axis=1)
    yt_ref[...] += jnp.dot(vt_ref[...], wg, preferred_element_type=f32)

    @pl.when(j == pl.num_programs(1) - 1)
    def _():
        y = yt_ref[...].T
        ms = jnp.mean(y * y, axis=-1, keepdims=True)
        o_ref[...] = lat_ref[...] + g2_ref[0] * (y * lax.rsqrt(ms + EPS) * nw_ref[...])


def _peer(h2, lat, wq, keys, u, vt, g2, nw, *, tokens_per_batch, tt, ec):
    N, D = h2.shape
    E = u.shape[0]
    tiles_per_batch = tokens_per_batch // tt
    sel_width = 2 if (tt // LANE) % 2 == 0 else 1
    sel = lambda dt: pltpu.VMEM((PEER_HEADS, tt // LANE, PEER_NKEYS, LANE), dt)
    return pl.pallas_call(
        _peer_kernel,
        name="peer",
        out_shape=jax.ShapeDtypeStruct((N, D), f32),
        grid=(N // tt, E // ec),
        in_specs=[pl.BlockSpec((tt, D), lambda i, j: (i, 0)),
                  pl.BlockSpec((tt, D), lambda i, j: (i, 0)),
                  pl.BlockSpec(wq.shape, lambda i, j: (0, 0, 0)),
                  pl.BlockSpec(keys.shape, lambda i, j: (0, 0, 0, 0)),
                  pl.BlockSpec((ec, D), lambda i, j: (j, 0)),
                  pl.BlockSpec((D, ec), lambda i, j: (0, j)),
                  pl.BlockSpec((1, 1, D), lambda i, j: (i // tiles_per_batch, 0, 0)),
                  pl.BlockSpec((1, D), lambda i, j: (0, 0))],
        out_specs=pl.BlockSpec((tt, D), lambda i, j: (i, 0)),
        scratch_shapes=[sel(bf16), sel(bf16), sel(f32), sel(f32),
                        pltpu.VMEM((tt, PEER_DQ), f32),
                        pltpu.VMEM((sel_width, PEER_TOPK, LANE), f32),
                        pltpu.VMEM((sel_width, PEER_TOPK, LANE), f32),
                        pltpu.VMEM((sel_width, CHUNK, LANE), f32),
                        pltpu.VMEM((tt // LANE, ec, LANE), bf16),
                        pltpu.VMEM((D, tt), f32)],
        compiler_params=_params(("arbitrary", "arbitrary")),
    )(h2, lat, wq, keys, u, vt, g2, nw)


def _dir_rows(t, n_heads, width=CHUNK):
    B, L, _ = t.shape
    return t.reshape(B, L, N_DIR, n_heads).transpose(0, 2, 3, 1).reshape(B, N_DIR, n_heads, L // width, width)


def _to_colmajor(t, rows):
    B, _, C = t.shape
    return t.reshape(B, rows, GRID_W, C).transpose(0, 2, 1, 3).reshape(B, GRID_W * rows, C)


def _pick(n, prefs):
    for p in prefs:
        if n % p == 0:
            return p
    raise ValueError(f"no tile in {prefs} divides {n}")


def kernel(x, c, ctx, c_ctx, w_mod, b_mod, norm_w, w_in, ml_gate_bias, dn_a_log, dn_dt_bias, dn_conv_w,
           ml_norm_w, dn_norm_w, w_branch_a, w_branch_b, w_out, peer_wq, peer_keys, peer_u, peer_v):
    B, L, D = x.shape
    Lc = ctx.shape[1]
    assert w_mod.shape[0] == 1, "single-layer block"
    assert L % GRID_W == 0 and L % CHUNK == 0 and Lc % CHUNK == 0
    rows = L // GRID_W
    assert rows % 8 == 0 and CHUNK % rows == 0
    assert D == ML_HEADS * ML_DV == DN_HEADS * DN_DV

    mod = _modulation(jnp.concatenate([c, c_ctx[None, :]], axis=0), w_mod[0], b_mod[0])
    mod = mod.reshape(B + 1, N_MOD, 1, D)
    sh1, sc1, g1, sh2, sc2, g2 = (mod[:B, i] for i in range(N_MOD))
    sh1c, sc1c = mod[B:, 0], mod[B:, 1]

    w = w_in[0]
    sizes = (512, 512, 1024, 1024, 8, 8, 1024, 1024, 1024, 1024, 16, 16, 1024, 1024)
    offs = [0]
    for s in sizes:
        offs.append(offs[-1] + s)
    col = lambda i: w[:, offs[i]:offs[i + 1]]
    w_main = jnp.concatenate([col(0), col(1), col(2), col(3), col(9), col(12), col(13)], axis=1).astype(bf16)
    w_dn = jnp.concatenate([col(6), col(7), col(8)], axis=1).astype(bf16)
    w_gate = jnp.concatenate([col(4), col(5), col(10), col(11)], axis=1)
    n_gate = w_gate.shape[1]
    w_gate = jnp.pad(w_gate, ((0, 0), (0, LANE - n_gate))).astype(bf16)

    nw = norm_w[0]
    tt_lat = _pick(L, (1024, 512, 256, 128, 64))
    tt_ctx = _pick(Lc, (256, 128, 64))
    proj = lambda xs, sc_, sh_, tt: (
        _project(xs, nw[0:1], sc_, sh_, w_main, tn=1024, tt=tt, head_major=False, out_dtype=bf16),
        _project(xs, nw[0:1], sc_, sh_, w_dn, tn=1024, tt=tt, head_major=True, out_dtype=f32),
        _project(xs, nw[0:1], sc_, sh_, w_gate, tn=LANE, tt=tt, head_major=False, out_dtype=f32))
    p_lat, dn_lat, g_lat = proj(x, sc1, sh1, tt_lat)
    p_ctx, dn_ctx, g_ctx = proj(ctx, sc1c, sh1c, tt_ctx)

    nh = N_DIR * ML_HEADS
    nd = N_DIR * DN_HEADS
    gi, gf = _dir_rows(g_lat[..., 0:nh], ML_HEADS), _dir_rows(g_lat[..., nh:2 * nh], ML_HEADS)
    gic, gfc = _dir_rows(g_ctx[..., 0:nh], ML_HEADS), _dir_rows(g_ctx[..., nh:2 * nh], ML_HEADS)
    dn_width = CHUNK * math.gcd(math.gcd(L // CHUNK, Lc // CHUNK), 4)
    be = _dir_rows(_to_colmajor(g_lat[..., 2 * nh:2 * nh + nd], rows), DN_HEADS, dn_width)
    ar = _dir_rows(_to_colmajor(g_lat[..., 2 * nh + nd:2 * nh + 2 * nd], rows), DN_HEADS, dn_width)
    bec = _dir_rows(g_ctx[..., 2 * nh:2 * nh + nd], DN_HEADS, dn_width)
    arc = _dir_rows(g_ctx[..., 2 * nh + nd:2 * nh + 2 * nd], DN_HEADS, dn_width)
    bias = jnp.broadcast_to(ml_gate_bias[0].astype(f32)[:, :, :, None, None], (2, N_DIR, ML_HEADS, 1, CHUNK))
    alog = jnp.broadcast_to(dn_a_log[0].astype(f32)[:, :, None, None], (N_DIR, DN_HEADS, 1, dn_width))
    dtb = jnp.broadcast_to(dn_dt_bias[0].astype(f32)[:, :, None, None], (N_DIR, DN_HEADS, 1, dn_width))

    hm = _mlstm(p_lat, p_ctx, gi, gf, gic, gfc, bias)

    taps = dn_conv_w[0].astype(f32).reshape(DN_CONV, 3 * DN_HEADS, LANE).transpose(1, 0, 2)
    taps = jnp.pad(taps, ((0, 0), (0, 8 - DN_CONV), (0, 0)))
    qkv_lat = _dn_prep(dn_lat, taps, seq_len=rows, n_cols=GRID_W, heads_per_step=2)
    qkv_ctx = _dn_prep(dn_ctx, taps, seq_len=Lc, n_cols=1, heads_per_step=DN_HEADS)
    hd = _gdn(qkv_lat, qkv_ctx, be, ar, bec, arc, alog, dtb, rows=rows)

    lat1, h2 = _merge(x, hm, hd, p_lat, ml_norm_w[0].reshape(1, D).astype(f32),
                      jnp.tile(dn_norm_w[0].astype(f32), DN_HEADS).reshape(1, D),
                      w_branch_a[0].astype(bf16), w_branch_b[0].astype(bf16), w_out[0].astype(bf16),
                      nw[1:2], nw[2:3], g1, sc2, sh2, tt=_pick(L, (256, 128, 64)))

    wq = peer_wq[0].astype(bf16).reshape(D, PEER_HEADS, PEER_DQ).transpose(1, 0, 2)
    out = _peer(h2.reshape(B * L, D), lat1.reshape(B * L, D), wq, peer_keys[0].astype(bf16),
                peer_u[0].astype(bf16), _transposed_bf16(peer_v[0], te=1024), g2, nw[3:4],
                tokens_per_batch=L, tt=_pick(L, (512, 256)), ec=2048)
    return out.reshape(B, L, D)
```

```python
import functools
import math

import jax
import jax.numpy as jnp
from jax import lax
from jax.experimental import pallas as pl
from jax.experimental.pallas import tpu as pltpu

f32 = jnp.float32
bf16 = jnp.bfloat16

EPS = 1e-6
GRID_W = 64
N_DIR = 2
N_MOD = 6
ML_HEADS, ML_DQK, ML_DV = 4, 128, 256
DN_HEADS, DN_DK, DN_DV, DN_CONV = 8, 128, 128, 5
CHUNK = 64
PEER_HEADS, PEER_NKEYS, PEER_DQ, PEER_TOPK = 8, 128, 256, 16
LANE = 128
VMEM_LIMIT = 56 * 1024 * 1024

NT = (((1,), (1,)), ((), ()))
TN = (((0,), (0,)), ((), ()))
NEG_INF = float("-inf")


def _params(sem, flags=None):
    return pltpu.CompilerParams(dimension_semantics=sem, vmem_limit_bytes=VMEM_LIMIT, flags=flags)


def _mod_kernel(c_ref, w_ref, b_ref, o_ref):
    c = c_ref[...]
    s = c * jax.nn.sigmoid(c)
    o_ref[...] = jnp.dot(s, w_ref[...], preferred_element_type=f32) + b_ref[...]


def _modulation(cond, w_mod, b_mod):
    n, d = cond.shape
    return pl.pallas_call(
        _mod_kernel,
        name="modulation",
        out_shape=jax.ShapeDtypeStruct((n, N_MOD * d), f32),
        grid=(N_MOD,),
        in_specs=[pl.BlockSpec((n, d), lambda j: (0, 0)),
                  pl.BlockSpec((d, d), lambda j: (0, j)),
                  pl.BlockSpec((1, d), lambda j: (0, j))],
        out_specs=pl.BlockSpec((n, d), lambda j: (0, j)),
        compiler_params=_params(("arbitrary",)),
    )(cond, w_mod, b_mod.reshape(1, -1))


def _proj_kernel(x_ref, nw_ref, sc_ref, sh_ref, w_ref, o_ref, h_ref, *, head_major):
    @pl.when(pl.program_id(2) == 0)
    def _():
        x = x_ref[0]
        ms = jnp.mean(x * x, axis=-1, keepdims=True)
        h = (x * lax.rsqrt(ms + EPS)) * nw_ref[...]
        h_ref[...] = (h * (1.0 + sc_ref[0]) + sh_ref[0]).astype(bf16)

    acc = jnp.dot(h_ref[...], w_ref[...], preferred_element_type=f32).astype(o_ref.dtype)
    if head_major:
        for i in range(acc.shape[1] // LANE):
            o_ref[0, i] = acc[:, i * LANE:(i + 1) * LANE]
    else:
        o_ref[0] = acc


def _project(xs, nw, sc, sh, w, *, tn, tt, head_major, out_dtype):
    B, L, D = xs.shape
    N = w.shape[1]
    per_batch = sc.shape[0] == B and B > 1
    mod_map = (lambda b, t, j: (b, 0, 0)) if per_batch else (lambda b, t, j: (0, 0, 0))
    if head_major:
        out_shape = jax.ShapeDtypeStruct((B, N // LANE, L, LANE), out_dtype)
        out_spec = pl.BlockSpec((1, tn // LANE, tt, LANE), lambda b, t, j: (b, j, t, 0))
    else:
        out_shape = jax.ShapeDtypeStruct((B, L, N), out_dtype)
        out_spec = pl.BlockSpec((1, tt, tn), lambda b, t, j: (b, t, j))
    return pl.pallas_call(
        functools.partial(_proj_kernel, head_major=head_major),
        name="in_proj",
        out_shape=out_shape,
        grid=(B, L // tt, N // tn),
        in_specs=[pl.BlockSpec((1, tt, D), lambda b, t, j: (b, t, 0)),
                  pl.BlockSpec((1, D), lambda b, t, j: (0, 0)),
                  pl.BlockSpec((1, 1, D), mod_map),
                  pl.BlockSpec((1, 1, D), mod_map),
                  pl.BlockSpec((D, tn), lambda b, t, j: (0, j))],
        out_specs=out_spec,
        scratch_shapes=[pltpu.VMEM((tt, D), bf16)],
        compiler_params=_params(("arbitrary", "arbitrary", "arbitrary")),
    )(xs, nw, sc, sh, w)


def _chunk_masks(reverse, n=1):
    size = n * CHUNK
    row = lax.broadcasted_iota(jnp.int32, (size, size), 0)
    col = lax.broadcasted_iota(jnp.int32, (size, size), 1)
    eye = row == col
    same = (row >> 6) == (col >> 6) if n > 1 else None
    both = (lambda m: jnp.logical_and(same, m)) if n > 1 else (lambda m: m)
    if reverse:
        return eye, both(col >= row), both(row >= col), same
    return eye, both(col <= row), both(row <= col), same


def _row_to_col(v_row, eye):
    return jnp.sum(jnp.where(eye, jnp.broadcast_to(v_row, eye.shape), 0.0), axis=1, keepdims=True)


def _cumsum_forms(v_row, eye, incl, incl_t):
    vb = jnp.broadcast_to(v_row, eye.shape)
    c_col = jnp.sum(jnp.where(incl, vb, 0.0), axis=1, keepdims=True)
    v_col = jnp.sum(jnp.where(eye, vb, 0.0), axis=1, keepdims=True)
    c_row = jnp.sum(jnp.where(incl_t, v_col, 0.0), axis=0, keepdims=True)
    return c_col, c_row


def _ml_chunks(problems, c_ref, n_ref, want_out):
    dirs = range(len(problems))
    gate = []
    for qs, k, v, ig_row, lf_row, m, masks in problems:
        eye, incl, incl_t, _ = masks
        b_col, b_row = _cumsum_forms(lf_row, eye, incl, incl_t)
        b_last = jnp.sum(lf_row, axis=1, keepdims=True)
        g_row = b_last - b_row + ig_row
        m_chunk = jnp.max(g_row, axis=1, keepdims=True)
        e_col = _row_to_col(jnp.exp(g_row - m_chunk), eye)
        ek = k * e_col
        gate.append((b_col, b_row, b_last, m_chunk, ek))
    v16 = [p[2].astype(bf16) for p in problems]
    kv = [lax.dot_general(gate[d][4].astype(bf16), v16[d], TN, preferred_element_type=f32) for d in dirs]
    c_old = [c_ref[d] for d in dirs]
    n_old = [n_ref[d] for d in dirs]
    hs = [None for _ in dirs]
    if want_out:
        q16 = [p[0].astype(bf16) for p in problems]
        qk = [lax.dot_general(q16[d], problems[d][1].astype(bf16), NT, preferred_element_type=f32) for d in dirs]
        qc = [jnp.dot(q16[d], c_old[d].astype(bf16), preferred_element_type=f32) for d in dirs]
        w_in, m_in = [], []
        for d in dirs:
            b_col, b_row = gate[d][0], gate[d][1]
            dlog = jnp.where(problems[d][6][1], b_col - b_row + problems[d][3], NEG_INF)
            mi = jnp.max(dlog, axis=1, keepdims=True)
            m_in.append(mi)
            w_in.append(jnp.exp(dlog - mi) * qk[d])
        num_in = [jnp.dot(w_in[d].astype(bf16), v16[d], preferred_element_type=f32) for d in dirs]
        for d in dirs:
            qs, m = problems[d][0], problems[d][5]
            den_in = jnp.sum(w_in[d], axis=1, keepdims=True)
            m_inter = gate[d][0] + m
            m_t = jnp.maximum(m_inter, m_in[d])
            a = jnp.exp(m_inter - m_t)
            r = jnp.exp(m_in[d] - m_t)
            num = a * qc[d] + r * num_in[d]
            den = a * jnp.sum(qs * n_old[d], axis=1, keepdims=True) + r * den_in
            hs[d] = num / jnp.maximum(jnp.abs(den), jnp.exp(-m_t))
    ms = []
    for d in dirs:
        _, _, b_last, m_chunk, ek = gate[d]
        m = problems[d][5]
        m_new = jnp.maximum(b_last + m, m_chunk)
        sp = jnp.exp(b_last + m - m_new)
        sc = jnp.exp(m_chunk - m_new)
        c_ref[d] = sp * c_old[d] + sc * kv[d]
        n_ref[d] = sp * n_old[d] + sc * jnp.sum(ek, axis=0, keepdims=True)
        ms.append(m_new)
    return hs, ms


def _mlstm_kernel(q_ref, k_ref, v_ref, kc_ref, vc_ref, gi_ref, gf_ref, gic_ref, gfc_ref, bias_ref,
                  o_ref, c_ref, n_ref):
    n_lat = q_ref.shape[1] // CHUNK
    n_ctx = kc_ref.shape[1] // CHUNK
    scale = ML_DQK ** -0.5
    c_ref[...] = jnp.zeros_like(c_ref)
    n_ref[...] = jnp.zeros_like(n_ref)
    o_ref[...] = jnp.zeros_like(o_ref)
    masks = [_chunk_masks(False), _chunk_masks(True)]

    n_heads = q_ref.shape[2] // ML_DQK
    chains = [(hh, d) for hh in range(n_heads) for d in range(N_DIR)]
    qk_cols = lambda hh: slice(hh * ML_DQK, (hh + 1) * ML_DQK)
    v_cols = lambda hh: slice(hh * ML_DV, (hh + 1) * ML_DV)

    def gates(i_ref, f_ref, hh, d, c):
        ig = i_ref[0, d, hh, pl.ds(c, 1), :] + bias_ref[0, d, hh]
        lf = jax.nn.log_sigmoid(f_ref[0, d, hh, pl.ds(c, 1), :] + bias_ref[1, d, hh])
        return ig, lf

    def ctx_body(i, ms):
        problems = []
        for n, (hh, d) in enumerate(chains):
            c = i if d == 0 else n_ctx - 1 - i
            sl = pl.ds(pl.multiple_of(c * CHUNK, CHUNK), CHUNK)
            ig, lf = gates(gic_ref, gfc_ref, hh, d, c)
            problems.append((None, kc_ref[0, sl, qk_cols(hh)].astype(f32), vc_ref[0, sl, v_cols(hh)], ig, lf,
                             ms[n], masks[d]))
        _, out = _ml_chunks(problems, c_ref, n_ref, False)
        return tuple(out)

    def lat_body(i, ms):
        problems, slices = [], []
        for n, (hh, d) in enumerate(chains):
            c = i if d == 0 else n_lat - 1 - i
            sl = pl.ds(pl.multiple_of(c * CHUNK, CHUNK), CHUNK)
            ig, lf = gates(gi_ref, gf_ref, hh, d, c)
            problems.append((q_ref[0, sl, qk_cols(hh)].astype(f32) * scale, k_ref[0, sl, qk_cols(hh)].astype(f32),
                             v_ref[0, sl, v_cols(hh)], ig, lf, ms[n], masks[d]))
            slices.append(sl)
        hs, out = _ml_chunks(problems, c_ref, n_ref, True)
        for n, (hh, d) in enumerate(chains):
            o_ref[0, slices[n], v_cols(hh)] += hs[n]
        return tuple(out)

    m0 = tuple(jnp.zeros((1, 1), f32) for _ in chains)
    ms = lax.fori_loop(0, n_ctx, ctx_body, m0)
    lax.fori_loop(0, n_lat, lat_body, ms)


def _mlstm(p_lat, p_ctx, gi, gf, gic, gfc, bias):
    B, L, _ = p_lat.shape
    Lc = p_ctx.shape[1]
    n_lat, n_ctx = L // CHUNK, Lc // CHUNK
    hp = ML_HEADS
    qk_w, v_w = hp * ML_DQK, hp * ML_DV
    k_off = ML_HEADS * ML_DQK // qk_w
    v_off = 2 * ML_HEADS * ML_DQK // v_w
    gate_spec = lambda n: pl.BlockSpec((1, N_DIR, hp, n, CHUNK), lambda b, h: (b, 0, h, 0, 0))
    return pl.pallas_call(
        _mlstm_kernel,
        name="mlstm_scan",
        out_shape=jax.ShapeDtypeStruct((B, L, ML_HEADS * ML_DV), f32),
        grid=(B, ML_HEADS // hp),
        in_specs=[pl.BlockSpec((1, L, qk_w), lambda b, h: (b, 0, h)),
                  pl.BlockSpec((1, L, qk_w), lambda b, h: (b, 0, k_off + h)),
                  pl.BlockSpec((1, L, v_w), lambda b, h: (b, 0, v_off + h)),
                  pl.BlockSpec((1, Lc, qk_w), lambda b, h: (b, 0, k_off + h)),
                  pl.BlockSpec((1, Lc, v_w), lambda b, h: (b, 0, v_off + h)),
                  gate_spec(n_lat), gate_spec(n_lat), gate_spec(n_ctx), gate_spec(n_ctx),
                  pl.BlockSpec((2, N_DIR, hp, 1, CHUNK), lambda b, h: (0, 0, h, 0, 0))],
        out_specs=pl.BlockSpec((1, L, v_w), lambda b, h: (b, 0, h)),
        scratch_shapes=[pltpu.VMEM((hp * N_DIR, ML_DQK, ML_DV), f32), pltpu.VMEM((hp * N_DIR, 1, ML_DQK), f32)],
        compiler_params=_params(("arbitrary", "arbitrary")),
    )(p_lat, p_lat, p_lat, p_ctx, p_ctx, gi, gf, gic, gfc, bias)


def _dn_prep_kernel(x_ref, w_ref, o_ref, pad_ref, *, seq_len, n_cols):
    nh = x_ref.shape[1]
    kind = (pl.program_id(1) * nh) // DN_HEADS
    half = DN_CONV // 2
    total = seq_len * n_cols
    pad = (pad_ref.shape[0] - total) // 2
    pad_ref[0:pad, :] = jnp.zeros((pad, LANE), f32)
    pad_ref[pad + total:2 * pad + total, :] = jnp.zeros((pad, LANE), f32)
    q_scale = jnp.where(kind == 0, DN_DK ** -0.5, 1.0).astype(f32)
    for hh in range(nh):
        pad_ref[pad:pad + total, :] = x_ref[0, hh]
        y = jnp.zeros((total, LANE), f32)
        for t in range(DN_CONV):
            y = y + pad_ref[pl.ds(pad + (t - half) * n_cols, total), :] * w_ref[hh, t:t + 1, :]
        y = y * jax.nn.sigmoid(y)
        ss = jnp.sum(y * y, axis=-1, keepdims=True)
        y = y * jnp.where(kind == 2, 1.0, lax.rsqrt(ss + EPS) * q_scale)
        if n_cols > 1:
            y = pltpu.einshape("rcd->crd", y.reshape(seq_len, n_cols, LANE))
        o_ref[0, hh] = y.reshape(o_ref.shape[2:])


def _dn_prep(raw, wt, *, seq_len, n_cols, heads_per_step):
    B, G, L, _ = raw.shape
    n_chunks = L // CHUNK
    nh = heads_per_step
    assert DN_HEADS % nh == 0
    pad = -(-(DN_CONV // 2) * n_cols // 8) * 8
    return pl.pallas_call(
        functools.partial(_dn_prep_kernel, seq_len=seq_len, n_cols=n_cols),
        name="gdn_prep",
        out_shape=jax.ShapeDtypeStruct((B, G, n_chunks, CHUNK, LANE), f32),
        grid=(B, G // nh),
        in_specs=[pl.BlockSpec((1, nh, L, LANE), lambda b, g: (b, g, 0, 0)),
                  pl.BlockSpec((nh, 8, LANE), lambda b, g: (g, 0, 0))],
        out_specs=pl.BlockSpec((1, nh, n_chunks, CHUNK, LANE), lambda b, g: (b, g, 0, 0, 0)),
        scratch_shapes=[pltpu.VMEM((L + 2 * pad, LANE), f32)],
        compiler_params=_params(("arbitrary", "arbitrary")),
    )(raw, wt)


def _mm16(a, b):
    return jnp.dot(a.astype(bf16), b.astype(bf16), preferred_element_type=f32)


def _unit_tri_inverses(mats, eye):
    row = lax.broadcasted_iota(jnp.int32, eye.shape, 0)
    col = lax.broadcasted_iota(jnp.int32, eye.shape, 1)
    same = lambda bits: (row >> bits) == (col >> bits)
    m8 = [jnp.where(same(3), m, 0.0) for m in mats]
    p2 = [_mm16(m, m) for m in m8]
    p4 = [_mm16(p, p) for p in p2]
    inv = [eye.astype(f32) - m for m in m8]
    inv = [i + _mm16(i, p) for i, p in zip(inv, p2)]
    inv = [i + _mm16(i, p) for i, p in zip(inv, p4)]
    for bits in (3, 4, 5):
        joining = jnp.logical_and(same(bits + 1), jnp.logical_not(same(bits)))
        right = [_mm16(jnp.where(joining, m, 0.0), i) for m, i in zip(mats, inv)]
        inv = [i - _mm16(i, r) for i, r in zip(inv, right)]
    return inv


def _dn_prepare(problems):
    n = len(problems)
    pre = []
    for q, k, v, braw_row, araw_row, a_scale, dt_bias, masks in problems:
        eye, incl, incl_t, same = masks
        beta_col = _row_to_col(jax.nn.sigmoid(braw_row), eye)
        g_row = a_scale * jax.nn.softplus(araw_row + dt_bias)
        gb = jnp.broadcast_to(g_row, eye.shape)
        gc_col = jnp.sum(jnp.where(incl, gb, 0.0), axis=1, keepdims=True)
        g_col = jnp.sum(jnp.where(eye, gb, 0.0), axis=1, keepdims=True)
        gc_row = jnp.sum(jnp.where(incl_t, g_col, 0.0), axis=0, keepdims=True)
        if same is None:
            g_last = jnp.sum(g_row, axis=1, keepdims=True)
        else:
            g_last = jnp.sum(jnp.where(same, gb, 0.0), axis=1, keepdims=True)
        gam = jnp.exp(jnp.where(incl, gc_col - gc_row, NEG_INF))
        pre.append((beta_col, gc_col, g_last, gam, k * beta_col, k.astype(bf16)))
    eye = problems[0][7][0]
    gram = [lax.dot_general(pre[i][4].astype(bf16), pre[i][5], NT, preferred_element_type=f32) for i in range(n)]
    mats = []
    for i in range(n):
        _, incl, _, _ = problems[i][7]
        mats.append(jnp.where(jnp.logical_and(incl, jnp.logical_not(eye)), gram[i] * pre[i][3], 0.0))
    inv = [m.astype(bf16) for m in _unit_tri_inverses(mats, eye)]
    egc = [jnp.exp(pre[i][1]) for i in range(n)]
    rhs = [jnp.concatenate([problems[i][2] * pre[i][0], pre[i][4] * egc[i]], axis=1).astype(bf16) for i in range(n)]
    uw = [jnp.dot(inv[i], rhs[i], preferred_element_type=f32) for i in range(n)]
    u = [x[:, :DN_DV] for x in uw]
    w = [x[:, DN_DV:] for x in uw]
    a_qk = [None if problems[i][0] is None else
            lax.dot_general(problems[i][0].astype(bf16), pre[i][5], NT, preferred_element_type=f32) * pre[i][3]
            for i in range(n)]
    out = []
    for i in range(n):
        q, k = problems[i][0], problems[i][1]
        res = [u[i], w[i].astype(bf16), k * jnp.exp(pre[i][2] - pre[i][1]), jnp.exp(pre[i][2])]
        if q is not None:
            res += [(q * egc[i]).astype(bf16), a_qk[i].astype(bf16)]
        out.append(res)
    return out


def _gdn_kernel(q_ref, k_ref, v_ref, kc_ref, vc_ref, be_ref, ar_ref, bec_ref, arc_ref, alog_ref, dtb_ref,
                o_ref, s_ref, acc_ref, u_ref, w_ref, kd_ref, dl_ref, qg_ref, aq_ref, *, rows, group):
    n_lat = q_ref.shape[2]
    n_ctx = kc_ref.shape[2]
    size = group * CHUNK
    masks = [_chunk_masks(False, group), _chunk_masks(True, group)]

    n_heads = k_ref.shape[1]
    chains = [(hh, d) for hh in range(n_heads) for d in range(N_DIR)]

    def prepare(i, base, qr, kr, vr, ber, arr):
        problems = []
        blk = pl.ds(i * group, group)
        for hh, d in chains:
            q = None if qr is None else qr[0, hh, blk].reshape(size, DN_DK)
            k = kr[0, hh, blk].reshape(size, DN_DK)
            v = vr[0, hh, blk].reshape(size, DN_DV)
            problems.append((q, k, v, ber[0, d, hh, pl.ds(i, 1), :], arr[0, d, hh, pl.ds(i, 1), :],
                             -jnp.exp(alog_ref[d, hh]), dtb_ref[d, hh], masks[d]))
        results = _dn_prepare(problems)
        for n, res in enumerate(results):
            dst = pl.ds(base + i * group, group)
            u_ref[n, dst] = res[0].reshape(group, CHUNK, DN_DV)
            w_ref[n, dst] = res[1].reshape(group, CHUNK, DN_DK)
            dl = jnp.broadcast_to(res[3], (size, LANE))
            for g in range(group):
                c = base + i * group + g
                kd_ref[n, c] = res[2][g * CHUNK:(g + 1) * CHUNK].T.astype(bf16)
                dl_ref[n, c] = dl[g * CHUNK:g * CHUNK + 8]
            if qr is not None:
                qg_ref[n, pl.ds(i * group, group)] = res[4].reshape(group, CHUNK, DN_DK)
                for g in range(group):
                    aq_ref[n, i * group + g] = res[5][g * CHUNK:(g + 1) * CHUNK, g * CHUNK:(g + 1) * CHUNK]

    def ctx_prep(i, carry):
        prepare(i, 0, None, kc_ref, vc_ref, bec_ref, arc_ref)
        return carry

    def lat_prep(i, carry):
        prepare(i, n_ctx, q_ref, k_ref, v_ref, be_ref, ar_ref)
        return carry

    lax.fori_loop(0, n_ctx // group, ctx_prep, 0)
    lax.fori_loop(0, n_lat // group, lat_prep, 0)

    s_ref[...] = jnp.zeros_like(s_ref)
    acc_ref[...] = jnp.zeros_like(acc_ref)

    def step(cs, cls):
        idx = range(len(chains))
        s_old = [s_ref[n] for n in idx]
        s16 = [s.astype(bf16) for s in s_old]
        ws = [jnp.dot(w_ref[n, cs[chains[n][1]]], s16[n], preferred_element_type=f32) for n in idx]
        vn16 = [(u_ref[n, cs[chains[n][1]]] - ws[n]).astype(bf16) for n in idx]
        upd = [jnp.dot(kd_ref[n, cs[chains[n][1]]], vn16[n], preferred_element_type=f32) for n in idx]
        for n in idx:
            s_ref[n] = dl_ref[n, cs[chains[n][1]]][0:1, :] * s_old[n] + upd[n]
        if cls is not None:
            inter = [jnp.dot(qg_ref[n, cls[chains[n][1]]], s16[n], preferred_element_type=f32) for n in idx]
            intra = [jnp.dot(aq_ref[n, cls[chains[n][1]]], vn16[n], preferred_element_type=f32) for n in idx]
            for n in idx:
                hh, d = chains[n]
                acc_ref[hh, cls[d]] += inter[n] + intra[n]

    def ctx_step(i, carry):
        step((i, n_ctx - 1 - i), None)
        return carry

    def lat_step(i, carry):
        step((n_ctx + i, n_ctx + n_lat - 1 - i), (i, n_lat - 1 - i))
        return carry

    lax.fori_loop(0, n_ctx, ctx_step, 0)
    lax.fori_loop(0, n_lat, lat_step, 0)
    for hh in range(n_heads):
        o = acc_ref[hh].reshape(GRID_W, rows, DN_DV)
        o_ref[0, hh] = pltpu.einshape("crd->rcd", o).reshape(rows * GRID_W, DN_DV)


def _gdn(qkv_lat, qkv_ctx, be, ar, bec, arc, alog, dtb, *, rows):
    B, _, n_lat, _, _ = qkv_lat.shape
    n_ctx = qkv_ctx.shape[2]
    L = n_lat * CHUNK
    H = DN_HEADS
    n_all = n_ctx + n_lat
    width = be.shape[-1]
    group = width // CHUNK
    hp = 2
    nc = hp * N_DIR
    blk = lambda n, off: pl.BlockSpec((1, hp, n, CHUNK, LANE), lambda b, h: (b, off // hp + h, 0, 0, 0))
    gate_spec = lambda n: pl.BlockSpec((1, N_DIR, hp, n // group, width), lambda b, h: (b, 0, h, 0, 0))
    const_spec = pl.BlockSpec((N_DIR, hp, 1, width), lambda b, h: (0, h, 0, 0))
    out = pl.pallas_call(
        functools.partial(_gdn_kernel, rows=rows, group=group),
        name="gdn_scan",
        out_shape=jax.ShapeDtypeStruct((B, H, L, LANE), f32),
        grid=(B, H // hp),
        in_specs=[blk(n_lat, 0), blk(n_lat, H), blk(n_lat, 2 * H), blk(n_ctx, H), blk(n_ctx, 2 * H),
                  gate_spec(n_lat), gate_spec(n_lat), gate_spec(n_ctx), gate_spec(n_ctx),
                  const_spec, const_spec],
        out_specs=pl.BlockSpec((1, hp, L, LANE), lambda b, h: (b, h, 0, 0)),
        scratch_shapes=[pltpu.VMEM((nc, DN_DK, DN_DV), f32), pltpu.VMEM((hp, n_lat, CHUNK, DN_DV), f32),
                        pltpu.VMEM((nc, n_all, CHUNK, DN_DV), f32),
                        pltpu.VMEM((nc, n_all, CHUNK, DN_DK), bf16),
                        pltpu.VMEM((nc, n_all, DN_DK, CHUNK), bf16),
                        pltpu.VMEM((nc, n_all, 8, LANE), f32),
                        pltpu.VMEM((nc, n_lat, CHUNK, DN_DK), bf16),
                        pltpu.VMEM((nc, n_lat, CHUNK, CHUNK), bf16)],
        compiler_params=_params(("arbitrary", "arbitrary")),
    )(qkv_lat, qkv_lat, qkv_lat, qkv_ctx, qkv_ctx, be, ar, bec, arc, alog, dtb)
    return out


def _merge_kernel(x_ref, hm_ref, hd_ref, o_ref_in, z_ref, ga_ref, gb_ref, mlw_ref, dnw_ref, wa_ref, wb_ref,
                  wo_ref, nw1_ref, nw2_ref, g1_ref, sc2_ref, sh2_ref, lat_ref, h2_ref):
    hm = hm_ref[0]
    parts = []
    for h in range(ML_HEADS):
        seg = hm[:, h * ML_DV:(h + 1) * ML_DV]
        ms = jnp.mean(seg * seg, axis=-1, keepdims=True)
        parts.append(seg * lax.rsqrt(ms + EPS))
    ym = jnp.concatenate(parts, axis=-1) * mlw_ref[...] * jax.nn.sigmoid(o_ref_in[0].astype(f32))
    parts = []
    for h in range(DN_HEADS):
        seg = hd_ref[0, h]
        ms = jnp.mean(seg * seg, axis=-1, keepdims=True)
        parts.append(seg * lax.rsqrt(ms + EPS))
    z = z_ref[0].astype(f32)
    yd = jnp.concatenate(parts, axis=-1) * dnw_ref[...] * (z * jax.nn.sigmoid(z))
    ya = jnp.dot(ym.astype(bf16), wa_ref[...], preferred_element_type=f32)
    yb = jnp.dot(yd.astype(bf16), wb_ref[...], preferred_element_type=f32)
    y = jax.nn.sigmoid(ga_ref[0].astype(f32)) * ya + jax.nn.sigmoid(gb_ref[0].astype(f32)) * yb
    ymix = jnp.dot(y.astype(bf16), wo_ref[...], preferred_element_type=f32)
    ms = jnp.mean(ymix * ymix, axis=-1, keepdims=True)
    lat = x_ref[0] + g1_ref[0] * (ymix * lax.rsqrt(ms + EPS) * nw1_ref[...])
    lat_ref[0] = lat
    ms = jnp.mean(lat * lat, axis=-1, keepdims=True)
    h2 = (lat * lax.rsqrt(ms + EPS) * nw2_ref[...]) * (1.0 + sc2_ref[0]) + sh2_ref[0]
    h2_ref[0] = h2.astype(bf16)


def _merge(x, hm, hd, p_lat, mlw, dnw, wa, wb, wo, nw1, nw2, g1, sc2, sh2, *, tt):
    B, L, D = x.shape
    tok = lambda off: pl.BlockSpec((1, tt, D), lambda b, t: (b, t, off))
    full = lambda a: pl.BlockSpec(a.shape, lambda b, t: (0,) * a.ndim)
    per_b = pl.BlockSpec((1, 1, D), lambda b, t: (b, 0, 0))
    return pl.pallas_call(
        _merge_kernel,
        name="merge",
        out_shape=(jax.ShapeDtypeStruct((B, L, D), f32), jax.ShapeDtypeStruct((B, L, D), bf16)),
        grid=(B, L // tt),
        in_specs=[tok(0), tok(0),
                  pl.BlockSpec((1, DN_HEADS, tt, LANE), lambda b, t: (b, 0, t, 0)),
                  tok(2), tok(3), tok(4), tok(5),
                  full(mlw), full(dnw), full(wa), full(wb), full(wo), full(nw1), full(nw2),
                  per_b, per_b, per_b],
        out_specs=(tok(0), tok(0)),
        compiler_params=_params(("arbitrary", "arbitrary")),
    )(x, hm, hd, p_lat, p_lat, p_lat, p_lat, mlw, dnw, wa, wb, wo, nw1, nw2, g1, sc2, sh2)


def _transpose_kernel(x_ref, o_ref):
    o_ref[...] = x_ref[...].T.astype(o_ref.dtype)


def _transposed_bf16(x, *, te):
    E, D = x.shape
    return pl.pallas_call(
        _transpose_kernel,
        name="table_transpose",
        out_shape=jax.ShapeDtypeStruct((D, E), bf16),
        grid=(E // te,),
        in_specs=[pl.BlockSpec((te, D), lambda i: (i, 0))],
        out_specs=pl.BlockSpec((D, te), lambda i: (0, i)),
        compiler_params=_params(("arbitrary",)),
    )(x)


def _cand_pairs():
    return [(i, j) for i in range(PEER_TOPK) for j in range(PEER_TOPK) if (i + 1) * (j + 1) <= PEER_TOPK]


def _sorting_network(n):
    pairs = []
    p = 1
    while p < n:
        k = p
        while k >= 1:
            for j in range(k % p, n - k, 2 * k):
                for i in range(min(k, n - j - k)):
                    if (i + j) // (2 * p) == (i + j + k) // (2 * p):
                        pairs.append((i + j, i + j + k))
            k //= 2
        p *= 2
    return pairs


def _top_sorted(arrays, count):
    lists = []
    for x in arrays:
        n = x.shape[0] // 8
        xr = x.reshape(n, 8, x.shape[1])
        lists.append([xr[v] for v in range(n)])
    for i, j in _sorting_network(len(lists[0])):
        for tiles in lists:
            tiles[i], tiles[j] = jnp.maximum(tiles[i], tiles[j]), jnp.minimum(tiles[i], tiles[j])
    rows = [[] for _ in lists]
    for r in range(count):
        heads = [jnp.max(tiles[0], axis=0, keepdims=True) for tiles in lists]
        for k, m in enumerate(heads):
            rows[k].append(m)
        left = count - 1 - r
        for tiles, m in zip(lists, heads):
            hit = tiles[0] == m
            for k in range(min(left, len(tiles))):
                below = tiles[k + 1] if k + 1 < len(tiles) else NEG_INF
                tiles[k] = jnp.where(hit, below, tiles[k])
    return rows


def _peer_select(h2_ref, wq_ref, keys_ref, rank_ref, f2_ref, cnt_ref, e1_ref, qh_ref, a_ref, b_ref, cand_ref):
    n_tiles = h2_ref.shape[0] // LANE
    width = a_ref.shape[0]
    pairs = _cand_pairs()
    half = PEER_DQ // 2
    top = range(PEER_TOPK)

    def head_body(h, carry):
        qh_ref[...] = jnp.dot(h2_ref[...], wq_ref[h], preferred_element_type=f32)

        def tiles_body(g, carry2):
            lanes = range(width)
            tiles = [g * width + i for i in lanes]
            qt = [qh_ref[pl.ds(pl.multiple_of(t * LANE, LANE), LANE), :].astype(bf16) for t in tiles]
            s1 = [lax.dot_general(keys_ref[h, 0], q[:, :half], NT, preferred_element_type=f32) for q in qt]
            s2 = [lax.dot_general(keys_ref[h, 1], q[:, half:], NT, preferred_element_type=f32) for q in qt]
            tops = _top_sorted([s1[i] for i in lanes] + [s2[i] for i in lanes], PEER_TOPK)
            for i in lanes:
                for r in top:
                    a_ref[i, r:r + 1, :] = tops[i][r]
                    b_ref[i, r:r + 1, :] = tops[width + i][r]
            rank = [jnp.full(s.shape, float(PEER_TOPK), f32) for s in s2]
            for r in reversed(top):
                rank = [jnp.where(s2[i] >= b_ref[i, r:r + 1, :], float(r), rank[i]) for i in lanes]
            cand_ref[...] = jnp.full(cand_ref.shape, NEG_INF, f32)
            for c, (i1, i2) in enumerate(pairs):
                for i in lanes:
                    cand_ref[i, c:c + 1, :] = a_ref[i, i1:i1 + 1, :] + b_ref[i, i2:i2 + 1, :]
            cand = [cand_ref[i] for i in lanes]
            tau = [rows[-1] for rows in _top_sorted(cand, PEER_TOPK)]
            a0 = [a_ref[i, 0:1, :] for i in lanes]
            b0 = [b_ref[i, 0:1, :] for i in lanes]
            z = [jnp.sum(jnp.where(cand[i] >= tau[i], jnp.exp(cand[i] - (a0[i] + b0[i])), 0.0), axis=0,
                         keepdims=True) for i in lanes]
            cnt = [jnp.zeros(s.shape, f32) for s in s1]
            for r in top:
                cnt = [jnp.where(s1[i] + b_ref[i, r:r + 1, :] >= tau[i], float(r + 1), cnt[i]) for i in lanes]
            for i in lanes:
                rank_ref[h, tiles[i]] = rank[i].astype(bf16)
                f2_ref[h, tiles[i]] = (jnp.exp(s2[i] - b0[i]) * (0.5 / z[i])).astype(bf16)
                cnt_ref[h, tiles[i]] = cnt[i]
                e1_ref[h, tiles[i]] = jnp.exp(s1[i] - a0[i])
            return carry2

        return lax.fori_loop(0, n_tiles // width, tiles_body, carry)

    lax.fori_loop(0, PEER_HEADS, head_body, 0)


def _peer_kernel(h2_ref, lat_ref, wq_ref, keys_ref, u_ref, vt_ref, g2_ref, nw_ref, o_ref,
                 rank_ref, f2_ref, cnt_ref, e1_ref, qh_ref, a_ref, b_ref, cand_ref, wg_ref, yt_ref):
    j = pl.program_id(1)
    n_blk = u_ref.shape[0] // PEER_NKEYS
    n_tiles = h2_ref.shape[0] // LANE
    n_parts = 2
    per = n_blk // n_parts

    @pl.when(j == 0)
    def _():
        _peer_select(h2_ref, wq_ref, keys_ref, rank_ref, f2_ref, cnt_ref, e1_ref, qh_ref, a_ref, b_ref, cand_ref)
        yt_ref[...] = jnp.zeros_like(yt_ref)

    def gate_group(g, carry):
        t = g // n_parts
        p = g % n_parts
        ws = [None] * per
        for h in range(PEER_HEADS):
            rk = rank_ref[h, t]
            f2 = f2_ref[h, t]
            for b in range(per):
                i1 = j * n_blk + p * per + b
                cn = cnt_ref[h, t, pl.ds(i1, 1), :].astype(bf16)
                ee = e1_ref[h, t, pl.ds(i1, 1), :].astype(bf16)
                term = jnp.where(rk < cn, f2, 0) * ee
                ws[b] = term if ws[b] is None else ws[b] + term
        for b in range(per):
            r0 = pl.multiple_of((p * per + b) * PEER_NKEYS, PEER_NKEYS)
            wg_ref[t, pl.ds(r0, PEER_NKEYS), :] = ws[b]
        return carry

    lax.fori_loop(0, n_tiles * n_parts, gate_group, 0)

    at = lax.dot_general(u_ref[...], h2_ref[...], NT, preferred_element_type=f32)
    act = (at * (1.0 + lax.erf(at * (2.0 ** -0.5)))).astype(bf16)
    wg = jnp.concatenate([wg_ref[t] * act[:, t * LANE:(t + 1) * LANE] for t in range(n_tiles)], axis=1)
    yt_ref[...] += jnp.dot(vt_ref[...], wg, preferred_element_type=f32)

    @pl.when(j == pl.num_programs(1) - 1)
    def _():
        y = yt_ref[...].T
        ms = jnp.mean(y * y, axis=-1, keepdims=True)
        o_ref[...] = lat_ref[...] + g2_ref[0] * (y * lax.rsqrt(ms + EPS) * nw_ref[...])


def _peer(h2, lat, wq, keys, u, vt, g2, nw, *, tokens_per_batch, tt, ec):
    N, D = h2.shape
    E = u.shape[0]
    tiles_per_batch = tokens_per_batch // tt
    sel_width = 2 if (tt // LANE) % 2 == 0 else 1
    sel = lambda dt: pltpu.VMEM((PEER_HEADS, tt // LANE, PEER_NKEYS, LANE), dt)
    return pl.pallas_call(
        _peer_kernel,
        name="peer",
        out_shape=jax.ShapeDtypeStruct((N, D), f32),
        grid=(N // tt, E // ec),
        in_specs=[pl.BlockSpec((tt, D), lambda i, j: (i, 0)),
                  pl.BlockSpec((tt, D), lambda i, j: (i, 0)),
                  pl.BlockSpec(wq.shape, lambda i, j: (0, 0, 0)),
                  pl.BlockSpec(keys.shape, lambda i, j: (0, 0, 0, 0)),
                  pl.BlockSpec((ec, D), lambda i, j: (j, 0)),
                  pl.BlockSpec((D, ec), lambda i, j: (0, j)),
                  pl.BlockSpec((1, 1, D), lambda i, j: (i // tiles_per_batch, 0, 0)),
                  pl.BlockSpec((1, D), lambda i, j: (0, 0))],
        out_specs=pl.BlockSpec((tt, D), lambda i, j: (i, 0)),
        scratch_shapes=[sel(bf16), sel(bf16), sel(f32), sel(f32),
                        pltpu.VMEM((tt, PEER_DQ), f32),
                        pltpu.VMEM((sel_width, PEER_TOPK, LANE), f32),
                        pltpu.VMEM((sel_width, PEER_TOPK, LANE), f32),
                        pltpu.VMEM((sel_width, CHUNK, LANE), f32),
                        pltpu.VMEM((tt // LANE, ec, LANE), bf16),
                        pltpu.VMEM((D, tt), f32)],
        compiler_params=_params(("arbitrary", "arbitrary")),
    )(h2, lat, wq, keys, u, vt, g2, nw)


def _dir_rows(t, n_heads, width=CHUNK):
    B, L, _ = t.shape
    return t.reshape(B, L, N_DIR, n_heads).transpose(0, 2, 3, 1).reshape(B, N_DIR, n_heads, L // width, width)


def _to_colmajor(t, rows):
    B, _, C = t.shape
    return t.reshape(B, rows, GRID_W, C).transpose(0, 2, 1, 3).reshape(B, GRID_W * rows, C)


def _pick(n, prefs):
    for p in prefs:
        if n % p == 0:
            return p
    raise ValueError(f"no tile in {prefs} divides {n}")


def kernel(x, c, ctx, c_ctx, w_mod, b_mod, norm_w, w_in, ml_gate_bias, dn_a_log, dn_dt_bias, dn_conv_w,
           ml_norm_w, dn_norm_w, w_branch_a, w_branch_b, w_out, peer_wq, peer_keys, peer_u, peer_v):
    B, L, D = x.shape
    Lc = ctx.shape[1]
    assert w_mod.shape[0] == 1, "single-layer block"
    assert L % GRID_W == 0 and L % CHUNK == 0 and Lc % CHUNK == 0
    rows = L // GRID_W
    assert rows % 8 == 0 and CHUNK % rows == 0
    assert D == ML_HEADS * ML_DV == DN_HEADS * DN_DV

    mod = _modulation(jnp.concatenate([c, c_ctx[None, :]], axis=0), w_mod[0], b_mod[0])
    mod = mod.reshape(B + 1, N_MOD, 1, D)
    sh1, sc1, g1, sh2, sc2, g2 = (mod[:B, i] for i in range(N_MOD))
    sh1c, sc1c = mod[B:, 0], mod[B:, 1]

    w = w_in[0]
    sizes = (512, 512, 1024, 1024, 8, 8, 1024, 1024, 1024, 1024, 16, 16, 1024, 1024)
    offs = [0]
    for s in sizes:
        offs.append(offs[-1] + s)
    col = lambda i: w[:, offs[i]:offs[i + 1]]
    w_main = jnp.concatenate([col(0), col(1), col(2), col(3), col(9), col(12), col(13)], axis=1).astype(bf16)
    w_dn = jnp.concatenate([col(6), col(7), col(8)], axis=1).astype(bf16)
    w_gate = jnp.concatenate([col(4), col(5), col(10), col(11)], axis=1)
    n_gate = w_gate.shape[1]
    w_gate = jnp.pad(w_gate, ((0, 0), (0, LANE - n_gate))).astype(bf16)

    nw = norm_w[0]
    tt_lat = _pick(L, (1024, 512, 256, 128, 64))
    tt_ctx = _pick(Lc, (256, 128, 64))
    proj = lambda xs, sc_, sh_, tt: (
        _project(xs, nw[0:1], sc_, sh_, w_main, tn=1024, tt=tt, head_major=False, out_dtype=bf16),
        _project(xs, nw[0:1], sc_, sh_, w_dn, tn=1024, tt=tt, head_major=True, out_dtype=f32),
        _project(xs, nw[0:1], sc_, sh_, w_gate, tn=LANE, tt=tt, head_major=False, out_dtype=f32))
    p_lat, dn_lat, g_lat = proj(x, sc1, sh1, tt_lat)
    p_ctx, dn_ctx, g_ctx = proj(ctx, sc1c, sh1c, tt_ctx)

    nh = N_DIR * ML_HEADS
    nd = N_DIR * DN_HEADS
    gi, gf = _dir_rows(g_lat[..., 0:nh], ML_HEADS), _dir_rows(g_lat[..., nh:2 * nh], ML_HEADS)
    gic, gfc = _dir_rows(g_ctx[..., 0:nh], ML_HEADS), _dir_rows(g_ctx[..., nh:2 * nh], ML_HEADS)
    dn_width = CHUNK * math.gcd(math.gcd(L // CHUNK, Lc // CHUNK), 4)
    be = _dir_rows(_to_colmajor(g_lat[..., 2 * nh:2 * nh + nd], rows), DN_HEADS, dn_width)
    ar = _dir_rows(_to_colmajor(g_lat[..., 2 * nh + nd:2 * nh + 2 * nd], rows), DN_HEADS, dn_width)
    bec = _dir_rows(g_ctx[..., 2 * nh:2 * nh + nd], DN_HEADS, dn_width)
    arc = _dir_rows(g_ctx[..., 2 * nh + nd:2 * nh + 2 * nd], DN_HEADS, dn_width)
    bias = jnp.broadcast_to(ml_gate_bias[0].astype(f32)[:, :, :, None, None], (2, N_DIR, ML_HEADS, 1, CHUNK))
    alog = jnp.broadcast_to(dn_a_log[0].astype(f32)[:, :, None, None], (N_DIR, DN_HEADS, 1, dn_width))
    dtb = jnp.broadcast_to(dn_dt_bias[0].astype(f32)[:, :, None, None], (N_DIR, DN_HEADS, 1, dn_width))

    hm = _mlstm(p_lat, p_ctx, gi, gf, gic, gfc, bias)

    taps = dn_conv_w[0].astype(f32).reshape(DN_CONV, 3 * DN_HEADS, LANE).transpose(1, 0, 2)
    taps = jnp.pad(taps, ((0, 0), (0, 8 - DN_CONV), (0, 0)))
    qkv_lat = _dn_prep(dn_lat, taps, seq_len=rows, n_cols=GRID_W, heads_per_step=2)
    qkv_ctx = _dn_prep(dn_ctx, taps, seq_len=Lc, n_cols=1, heads_per_step=DN_HEADS)
    hd = _gdn(qkv_lat, qkv_ctx, be, ar, bec, arc, alog, dtb, rows=rows)

    lat1, h2 = _merge(x, hm, hd, p_lat, ml_norm_w[0].reshape(1, D).astype(f32),
                      jnp.tile(dn_norm_w[0].astype(f32), DN_HEADS).reshape(1, D),
                      w_branch_a[0].astype(bf16), w_branch_b[0].astype(bf16), w_out[0].astype(bf16),
                      nw[1:2], nw[2:3], g1, sc2, sh2, tt=_pick(L, (256, 128, 64)))

    wq = peer_wq[0].astype(bf16).reshape(D, PEER_HEADS, PEER_DQ).transpose(1, 0, 2)
    out = _peer(h2.reshape(B * L, D), lat1.reshape(B * L, D), wq, peer_keys[0].astype(bf16),
                peer_u[0].astype(bf16), _transposed_bf16(peer_v[0], te=1024), g2, nw[3:4],
                tokens_per_batch=L, tt=_pick(L, (512, 256)), ec=2048)
    return out.reshape(B, L, D)
```

```python
import functools
import math

import jax
import jax.numpy as jnp
from jax import lax
from jax.experimental import pallas as pl
from jax.experimental.pallas import tpu as pltpu

f32 = jnp.float32
bf16 = jnp.bfloat16

EPS = 1e-6
GRID_W = 64
N_DIR = 2
N_MOD = 6
ML_HEADS, ML_DQK, ML_DV = 4, 128, 256
DN_HEADS, DN_DK, DN_DV, DN_CONV = 8, 128, 128, 5
CHUNK = 64
PEER_HEADS, PEER_NKEYS, PEER_DQ, PEER_TOPK = 8, 128, 256, 16
LANE = 128
VMEM_LIMIT = 56 * 1024 * 1024

NT = (((1,), (1,)), ((), ()))
TN = (((0,), (0,)), ((), ()))
NEG_INF = float("-inf")


def _params(sem, flags=None):
    return pltpu.CompilerParams(dimension_semantics=sem, vmem_limit_bytes=VMEM_LIMIT, flags=flags)


def _mod_kernel(c_ref, w_ref, b_ref, o_ref):
    c = c_ref[...]
    s = c * jax.nn.sigmoid(c)
    o_ref[...] = jnp.dot(s, w_ref[...], preferred_element_type=f32) + b_ref[...]


def _modulation(cond, w_mod, b_mod):
    n, d = cond.shape
    return pl.pallas_call(
        _mod_kernel,
        name="modulation",
        out_shape=jax.ShapeDtypeStruct((n, N_MOD * d), f32),
        grid=(N_MOD,),
        in_specs=[pl.BlockSpec((n, d), lambda j: (0, 0)),
                  pl.BlockSpec((d, d), lambda j: (0, j)),
                  pl.BlockSpec((1, d), lambda j: (0, j))],
        out_specs=pl.BlockSpec((n, d), lambda j: (0, j)),
        compiler_params=_params(("arbitrary",)),
    )(cond, w_mod, b_mod.reshape(1, -1))


def _proj_kernel(x_ref, nw_ref, sc_ref, sh_ref, w_ref, o_ref, h_ref, *, head_major):
    @pl.when(pl.program_id(2) == 0)
    def _():
        x = x_ref[0]
        ms = jnp.mean(x * x, axis=-1, keepdims=True)
        h = (x * lax.rsqrt(ms + EPS)) * nw_ref[...]
        h_ref[...] = (h * (1.0 + sc_ref[0]) + sh_ref[0]).astype(bf16)

    acc = jnp.dot(h_ref[...], w_ref[...], preferred_element_type=f32).astype(o_ref.dtype)
    if head_major:
        for i in range(acc.shape[1] // LANE):
            o_ref[0, i] = acc[:, i * LANE:(i + 1) * LANE]
    else:
        o_ref[0] = acc


def _project(xs, nw, sc, sh, w, *, tn, tt, head_major, out_dtype):
    B, L, D = xs.shape
    N = w.shape[1]
    per_batch = sc.shape[0] == B and B > 1
    mod_map = (lambda b, t, j: (b, 0, 0)) if per_batch else (lambda b, t, j: (0, 0, 0))
    if head_major:
        out_shape = jax.ShapeDtypeStruct((B, N // LANE, L, LANE), out_dtype)
        out_spec = pl.BlockSpec((1, tn // LANE, tt, LANE), lambda b, t, j: (b, j, t, 0))
    else:
        out_shape = jax.ShapeDtypeStruct((B, L, N), out_dtype)
        out_spec = pl.BlockSpec((1, tt, tn), lambda b, t, j: (b, t, j))
    return pl.pallas_call(
        functools.partial(_proj_kernel, head_major=head_major),
        name="in_proj",
        out_shape=out_shape,
        grid=(B, L // tt, N // tn),
        in_specs=[pl.BlockSpec((1, tt, D), lambda b, t, j: (b, t, 0)),
                  pl.BlockSpec((1, D), lambda b, t, j: (0, 0)),
                  pl.BlockSpec((1, 1, D), mod_map),
                  pl.BlockSpec((1, 1, D), mod_map),
                  pl.BlockSpec((D, tn), lambda b, t, j: (0, j))],
        out_specs=out_spec,
        scratch_shapes=[pltpu.VMEM((tt, D), bf16)],
        compiler_params=_params(("arbitrary", "arbitrary", "arbitrary")),
    )(xs, nw, sc, sh, w)


def _chunk_masks(reverse, n=1):
    size = n * CHUNK
    row = lax.broadcasted_iota(jnp.int32, (size, size), 0)
    col = lax.broadcasted_iota(jnp.int32, (size, size), 1)
    eye = row == col
    same = (row >> 6) == (col >> 6) if n > 1 else None
    both = (lambda m: jnp.logical_and(same, m)) if n > 1 else (lambda m: m)
    if reverse:
        return eye, both(col >= row), both(row >= col), same
    return eye, both(col <= row), both(row <= col), same


def _row_to_col(v_row, eye):
    return jnp.sum(jnp.where(eye, jnp.broadcast_to(v_row, eye.shape), 0.0), axis=1, keepdims=True)


def _cumsum_forms(v_row, eye, incl, incl_t):
    vb = jnp.broadcast_to(v_row, eye.shape)
    c_col = jnp.sum(jnp.where(incl, vb, 0.0), axis=1, keepdims=True)
    v_col = jnp.sum(jnp.where(eye, vb, 0.0), axis=1, keepdims=True)
    c_row = jnp.sum(jnp.where(incl_t, v_col, 0.0), axis=0, keepdims=True)
    return c_col, c_row


def _ml_chunks(problems, c_ref, n_ref, want_out):
    dirs = range(len(problems))
    gate = []
    for qs, k, v, ig_row, lf_row, m, masks in problems:
        eye, incl, incl_t, _ = masks
        b_col, b_row = _cumsum_forms(lf_row, eye, incl, incl_t)
        b_last = jnp.sum(lf_row, axis=1, keepdims=True)
        g_row = b_last - b_row + ig_row
        m_chunk = jnp.max(g_row, axis=1, keepdims=True)
        e_col = _row_to_col(jnp.exp(g_row - m_chunk), eye)
        ek = k * e_col
        gate.append((b_col, b_row, b_last, m_chunk, ek))
    v16 = [p[2].astype(bf16) for p in problems]
    kv = [lax.dot_general(gate[d][4].astype(bf16), v16[d], TN, preferred_element_type=f32) for d in dirs]
    c_old = [c_ref[d] for d in dirs]
    n_old = [n_ref[d] for d in dirs]
    hs = [None for _ in dirs]
    if want_out:
        q16 = [p[0].astype(bf16) for p in problems]
        qk = [lax.dot_general(q16[d], problems[d][1].astype(bf16), NT, preferred_element_type=f32) for d in dirs]
        qc = [jnp.dot(q16[d], c_old[d].astype(bf16), preferred_element_type=f32) for d in dirs]
        w_in, m_in = [], []
        for d in dirs:
            b_col, b_row = gate[d][0], gate[d][1]
            dlog = jnp.where(problems[d][6][1], b_col - b_row + problems[d][3], NEG_INF)
            mi = jnp.max(dlog, axis=1, keepdims=True)
            m_in.append(mi)
            w_in.append(jnp.exp(dlog - mi) * qk[d])
        num_in = [jnp.dot(w_in[d].astype(bf16), v16[d], preferred_element_type=f32) for d in dirs]
        for d in dirs:
            qs, m = problems[d][0], problems[d][5]
            den_in = jnp.sum(w_in[d], axis=1, keepdims=True)
            m_inter = gate[d][0] + m
            m_t = jnp.maximum(m_inter, m_in[d])
            a = jnp.exp(m_inter - m_t)
            r = jnp.exp(m_in[d] - m_t)
            num = a * qc[d] + r * num_in[d]
            den = a * jnp.sum(qs * n_old[d], axis=1, keepdims=True) + r * den_in
            hs[d] = num / jnp.maximum(jnp.abs(den), jnp.exp(-m_t))
    ms = []
    for d in dirs:
        _, _, b_last, m_chunk, ek = gate[d]
        m = problems[d][5]
        m_new = jnp.maximum(b_last + m, m_chunk)
        sp = jnp.exp(b_last + m - m_new)
        sc = jnp.exp(m_chunk - m_new)
        c_ref[d] = sp * c_old[d] + sc * kv[d]
        n_ref[d] = sp * n_old[d] + sc * jnp.sum(ek, axis=0, keepdims=True)
        ms.append(m_new)
    return hs, ms


def _mlstm_kernel(q_ref, k_ref, v_ref, kc_ref, vc_ref, gi_ref, gf_ref, gic_ref, gfc_ref, bias_ref,
                  o_ref, c_ref, n_ref):
    n_lat = q_ref.shape[1] // CHUNK
    n_ctx = kc_ref.shape[1] // CHUNK
    scale = ML_DQK ** -0.5
    c_ref[...] = jnp.zeros_like(c_ref)
    n_ref[...] = jnp.zeros_like(n_ref)
    o_ref[...] = jnp.zeros_like(o_ref)
    masks = [_chunk_masks(False), _chunk_masks(True)]

    n_heads = q_ref.shape[2] // ML_DQK
    chains = [(hh, d) for hh in range(n_heads) for d in range(N_DIR)]
    qk_cols = lambda hh: slice(hh * ML_DQK, (hh + 1) * ML_DQK)
    v_cols = lambda hh: slice(hh * ML_DV, (hh + 1) * ML_DV)

    def gates(i_ref, f_ref, hh, d, c):
        ig = i_ref[0, d, hh, pl.ds(c, 1), :] + bias_ref[0, d, hh]
        lf = jax.nn.log_sigmoid(f_ref[0, d, hh, pl.ds(c, 1), :] + bias_ref[1, d, hh])
        return ig, lf

    def ctx_body(i, ms):
        problems = []
        for n, (hh, d) in enumerate(chains):
            c = i if d == 0 else n_ctx - 1 - i
            sl = pl.ds(pl.multiple_of(c * CHUNK, CHUNK), CHUNK)
            ig, lf = gates(gic_ref, gfc_ref, hh, d, c)
            problems.append((None, kc_ref[0, sl, qk_cols(hh)].astype(f32), vc_ref[0, sl, v_cols(hh)], ig, lf,
                             ms[n], masks[d]))
        _, out = _ml_chunks(problems, c_ref, n_ref, False)
        return tuple(out)

    def lat_body(i, ms):
        problems, slices = [], []
        for n, (hh, d) in enumerate(chains):
            c = i if d == 0 else n_lat - 1 - i
            sl = pl.ds(pl.multiple_of(c * CHUNK, CHUNK), CHUNK)
            ig, lf = gates(gi_ref, gf_ref, hh, d, c)
            problems.append((q_ref[0, sl, qk_cols(hh)].astype(f32) * scale, k_ref[0, sl, qk_cols(hh)].astype(f32),
                             v_ref[0, sl, v_cols(hh)], ig, lf, ms[n], masks[d]))
            slices.append(sl)
        hs, out = _ml_chunks(problems, c_ref, n_ref, True)
        for n, (hh, d) in enumerate(chains):
            o_ref[0, slices[n], v_cols(hh)] += hs[n]
        return tuple(out)

    m0 = tuple(jnp.zeros((1, 1), f32) for _ in chains)
    ms = lax.fori_loop(0, n_ctx, ctx_body, m0)
    lax.fori_loop(0, n_lat, lat_body, ms, unroll=2)


def _mlstm(p_lat, p_ctx, gi, gf, gic, gfc, bias):
    B, L, _ = p_lat.shape
    Lc = p_ctx.shape[1]
    n_lat, n_ctx = L // CHUNK, Lc // CHUNK
    hp = ML_HEADS
    qk_w, v_w = hp * ML_DQK, hp * ML_DV
    k_off = ML_HEADS * ML_DQK // qk_w
    v_off = 2 * ML_HEADS * ML_DQK // v_w
    gate_spec = lambda n: pl.BlockSpec((1, N_DIR, hp, n, CHUNK), lambda b, h: (b, 0, h, 0, 0))
    return pl.pallas_call(
        _mlstm_kernel,
        name="mlstm_scan",
        out_shape=jax.ShapeDtypeStruct((B, L, ML_HEADS * ML_DV), f32),
        grid=(B, ML_HEADS // hp),
        in_specs=[pl.BlockSpec((1, L, qk_w), lambda b, h: (b, 0, h)),
                  pl.BlockSpec((1, L, qk_w), lambda b, h: (b, 0, k_off + h)),
                  pl.BlockSpec((1, L, v_w), lambda b, h: (b, 0, v_off + h)),
                  pl.BlockSpec((1, Lc, qk_w), lambda b, h: (b, 0, k_off + h)),
                  pl.BlockSpec((1, Lc, v_w), lambda b, h: (b, 0, v_off + h)),
                  gate_spec(n_lat), gate_spec(n_lat), gate_spec(n_ctx), gate_spec(n_ctx),
                  pl.BlockSpec((2, N_DIR, hp, 1, CHUNK), lambda b, h: (0, 0, h, 0, 0))],
        out_specs=pl.BlockSpec((1, L, v_w), lambda b, h: (b, 0, h)),
        scratch_shapes=[pltpu.VMEM((hp * N_DIR, ML_DQK, ML_DV), f32), pltpu.VMEM((hp * N_DIR, 1, ML_DQK), f32)],
        compiler_params=_params(("arbitrary", "arbitrary")),
    )(p_lat, p_lat, p_lat, p_ctx, p_ctx, gi, gf, gic, gfc, bias)


def _dn_prep_kernel(x_ref, w_ref, o_ref, pad_ref, *, seq_len, n_cols):
    nh = x_ref.shape[1]
    kind = (pl.program_id(1) * nh) // DN_HEADS
    half = DN_CONV // 2
    total = seq_len * n_cols
    pad = (pad_ref.shape[0] - total) // 2
    pad_ref[0:pad, :] = jnp.zeros((pad, LANE), f32)
    pad_ref[pad + total:2 * pad + total, :] = jnp.zeros((pad, LANE), f32)
    q_scale = jnp.where(kind == 0, DN_DK ** -0.5, 1.0).astype(f32)
    for hh in range(nh):
        pad_ref[pad:pad + total, :] = x_ref[0, hh]
        y = jnp.zeros((total, LANE), f32)
        for t in range(DN_CONV):
            y = y + pad_ref[pl.ds(pad + (t - half) * n_cols, total), :] * w_ref[hh, t:t + 1, :]
        y = y * jax.nn.sigmoid(y)
        ss = jnp.sum(y * y, axis=-1, keepdims=True)
        y = y * jnp.where(kind == 2, 1.0, lax.rsqrt(ss + EPS) * q_scale)
        if n_cols > 1:
            y = pltpu.einshape("rcd->crd", y.reshape(seq_len, n_cols, LANE))
        o_ref[0, hh] = y.reshape(o_ref.shape[2:]).astype(o_ref.dtype)


def _dn_prep(raw, wt, *, seq_len, n_cols, heads_per_step):
    B, G, L, _ = raw.shape
    n_chunks = L // CHUNK
    nh = heads_per_step
    assert DN_HEADS % nh == 0
    pad = -(-(DN_CONV // 2) * n_cols // 8) * 8
    return pl.pallas_call(
        functools.partial(_dn_prep_kernel, seq_len=seq_len, n_cols=n_cols),
        name="gdn_prep",
        out_shape=jax.ShapeDtypeStruct((B, G, n_chunks, CHUNK, LANE), bf16),
        grid=(B, G // nh),
        in_specs=[pl.BlockSpec((1, nh, L, LANE), lambda b, g: (b, g, 0, 0)),
                  pl.BlockSpec((nh, 8, LANE), lambda b, g: (g, 0, 0))],
        out_specs=pl.BlockSpec((1, nh, n_chunks, CHUNK, LANE), lambda b, g: (b, g, 0, 0, 0)),
        scratch_shapes=[pltpu.VMEM((L + 2 * pad, LANE), f32)],
        compiler_params=_params(("arbitrary", "arbitrary")),
    )(raw, wt)


def _mm16(a, b):
    return jnp.dot(a.astype(bf16), b.astype(bf16), preferred_element_type=f32)


def _unit_tri_inverses(mats, eye):
    row = lax.broadcasted_iota(jnp.int32, eye.shape, 0)
    col = lax.broadcasted_iota(jnp.int32, eye.shape, 1)
    same = lambda bits: (row >> bits) == (col >> bits)
    m8 = [jnp.where(same(3), m, 0.0) for m in mats]
    p2 = [_mm16(m, m) for m in m8]
    p4 = [_mm16(p, p) for p in p2]
    inv = [eye.astype(f32) - m for m in m8]
    inv = [i + _mm16(i, p) for i, p in zip(inv, p2)]
    inv = [i + _mm16(i, p) for i, p in zip(inv, p4)]
    for bits in (3, 4, 5):
        joining = jnp.logical_and(same(bits + 1), jnp.logical_not(same(bits)))
        right = [_mm16(jnp.where(joining, m, 0.0), i) for m, i in zip(mats, inv)]
        inv = [i - _mm16(i, r) for i, r in zip(inv, right)]
    return inv


def _dn_prepare(problems):
    n = len(problems)
    pre = []
    for q, k, v, braw_row, araw_row, a_scale, dt_bias, masks in problems:
        eye, incl, incl_t, same = masks
        beta_col = _row_to_col(jax.nn.sigmoid(braw_row), eye)
        g_row = a_scale * jax.nn.softplus(araw_row + dt_bias)
        gb = jnp.broadcast_to(g_row, eye.shape)
        gc_col = jnp.sum(jnp.where(incl, gb, 0.0), axis=1, keepdims=True)
        g_col = jnp.sum(jnp.where(eye, gb, 0.0), axis=1, keepdims=True)
        gc_row = jnp.sum(jnp.where(incl_t, g_col, 0.0), axis=0, keepdims=True)
        if same is None:
            g_last = jnp.sum(g_row, axis=1, keepdims=True)
        else:
            g_last = jnp.sum(jnp.where(same, gb, 0.0), axis=1, keepdims=True)
        gam = jnp.exp(jnp.where(incl, gc_col - gc_row, NEG_INF))
        pre.append((beta_col, gc_col, g_last, gam, k * beta_col, k.astype(bf16)))
    eye = problems[0][7][0]
    gram = [lax.dot_general(pre[i][4].astype(bf16), pre[i][5], NT, preferred_element_type=f32) for i in range(n)]
    mats = []
    for i in range(n):
        _, incl, _, _ = problems[i][7]
        mats.append(jnp.where(jnp.logical_and(incl, jnp.logical_not(eye)), gram[i] * pre[i][3], 0.0))
    inv = [m.astype(bf16) for m in _unit_tri_inverses(mats, eye)]
    egc = [jnp.exp(pre[i][1]) for i in range(n)]
    rhs = [jnp.concatenate([problems[i][2] * pre[i][0], pre[i][4] * egc[i]], axis=1).astype(bf16) for i in range(n)]
    uw = [jnp.dot(inv[i], rhs[i], preferred_element_type=f32) for i in range(n)]
    u = [x[:, :DN_DV] for x in uw]
    w = [x[:, DN_DV:] for x in uw]
    a_qk = [None if problems[i][0] is None else
            lax.dot_general(problems[i][0].astype(bf16), pre[i][5], NT, preferred_element_type=f32) * pre[i][3]
            for i in range(n)]
    out = []
    for i in range(n):
        q, k = problems[i][0], problems[i][1]
        res = [u[i], w[i].astype(bf16), k * jnp.exp(pre[i][2] - pre[i][1]), jnp.exp(pre[i][2])]
        if q is not None:
            res += [(q * egc[i]).astype(bf16), a_qk[i].astype(bf16)]
        out.append(res)
    return out


def _gdn_kernel(q_ref, k_ref, v_ref, kc_ref, vc_ref, be_ref, ar_ref, bec_ref, arc_ref, alog_ref, dtb_ref,
                o_ref, s_ref, acc_ref, u_ref, w_ref, kd_ref, dl_ref, qg_ref, aq_ref, *, rows, group):
    n_lat = q_ref.shape[2]
    n_ctx = kc_ref.shape[2]
    size = group * CHUNK
    masks = [_chunk_masks(False, group), _chunk_masks(True, group)]

    n_heads = k_ref.shape[1]
    chains = [(hh, d) for hh in range(n_heads) for d in range(N_DIR)]

    def prepare(i, base, qr, kr, vr, ber, arr):
        problems = []
        blk = pl.ds(i * group, group)
        for hh, d in chains:
            q = None if qr is None else qr[0, hh, blk].reshape(size, DN_DK).astype(f32)
            k = kr[0, hh, blk].reshape(size, DN_DK).astype(f32)
            v = vr[0, hh, blk].reshape(size, DN_DV).astype(f32)
            problems.append((q, k, v, ber[0, d, hh, pl.ds(i, 1), :], arr[0, d, hh, pl.ds(i, 1), :],
                             -jnp.exp(alog_ref[d, hh]), dtb_ref[d, hh], masks[d]))
        results = _dn_prepare(problems)
        for n, res in enumerate(results):
            dst = pl.ds(base + i * group, group)
            u_ref[n, dst] = res[0].reshape(group, CHUNK, DN_DV)
            w_ref[n, dst] = res[1].reshape(group, CHUNK, DN_DK)
            dl = jnp.broadcast_to(res[3], (size, LANE))
            for g in range(group):
                c = base + i * group + g
                kd_ref[n, c] = res[2][g * CHUNK:(g + 1) * CHUNK].T.astype(bf16)
                dl_ref[n, c] = dl[g * CHUNK:g * CHUNK + 8]
            if qr is not None:
                qg_ref[n, pl.ds(i * group, group)] = res[4].reshape(group, CHUNK, DN_DK)
                for g in range(group):
                    aq_ref[n, i * group + g] = res[5][g * CHUNK:(g + 1) * CHUNK, g * CHUNK:(g + 1) * CHUNK]

    def ctx_prep(i, carry):
        prepare(i, 0, None, kc_ref, vc_ref, bec_ref, arc_ref)
        return carry

    def lat_prep(i, carry):
        prepare(i, n_ctx, q_ref, k_ref, v_ref, be_ref, ar_ref)
        return carry

    lax.fori_loop(0, n_ctx // group, ctx_prep, 0)
    lax.fori_loop(0, n_lat // group, lat_prep, 0, unroll=2)

    s_ref[...] = jnp.zeros_like(s_ref)
    acc_ref[...] = jnp.zeros_like(acc_ref)

    def step(cs, cls):
        idx = range(len(chains))
        s_old = [s_ref[n] for n in idx]
        s16 = [s.astype(bf16) for s in s_old]
        ws = [jnp.dot(w_ref[n, cs[chains[n][1]]], s16[n], preferred_element_type=f32) for n in idx]
        vn16 = [(u_ref[n, cs[chains[n][1]]] - ws[n]).astype(bf16) for n in idx]
        upd = [jnp.dot(kd_ref[n, cs[chains[n][1]]], vn16[n], preferred_element_type=f32) for n in idx]
        for n in idx:
            s_ref[n] = dl_ref[n, cs[chains[n][1]]][0:1, :] * s_old[n] + upd[n]
        if cls is not None:
            inter = [jnp.dot(qg_ref[n, cls[chains[n][1]]], s16[n], preferred_element_type=f32) for n in idx]
            intra = [jnp.dot(aq_ref[n, cls[chains[n][1]]], vn16[n], preferred_element_type=f32) for n in idx]
            for n in idx:
                hh, d = chains[n]
                acc_ref[hh, cls[d]] += inter[n] + intra[n]

    def ctx_step(i, carry):
        step((i, n_ctx - 1 - i), None)
        return carry

    def lat_step(i, carry):
        step((n_ctx + i, n_ctx + n_lat - 1 - i), (i, n_lat - 1 - i))
        return carry

    lax.fori_loop(0, n_ctx, ctx_step, 0)
    lax.fori_loop(0, n_lat, lat_step, 0, unroll=4)
    for hh in range(n_heads):
        o = acc_ref[hh].reshape(GRID_W, rows, DN_DV)
        o_ref[0, hh] = pltpu.einshape("crd->rcd", o).reshape(rows * GRID_W, DN_DV)


def _gdn(qkv_lat, qkv_ctx, be, ar, bec, arc, alog, dtb, *, rows):
    B, _, n_lat, _, _ = qkv_lat.shape
    n_ctx = qkv_ctx.shape[2]
    L = n_lat * CHUNK
    H = DN_HEADS
    n_all = n_ctx + n_lat
    width = be.shape[-1]
    group = width // CHUNK
    hp = 2
    nc = hp * N_DIR
    blk = lambda n, off: pl.BlockSpec((1, hp, n, CHUNK, LANE), lambda b, h: (b, off // hp + h, 0, 0, 0))
    gate_spec = lambda n: pl.BlockSpec((1, N_DIR, hp, n // group, width), lambda b, h: (b, 0, h, 0, 0))
    const_spec = pl.BlockSpec((N_DIR, hp, 1, width), lambda b, h: (0, h, 0, 0))
    out = pl.pallas_call(
        functools.partial(_gdn_kernel, rows=rows, group=group),
        name="gdn_scan",
        out_shape=jax.ShapeDtypeStruct((B, H, L, LANE), f32),
        grid=(B, H // hp),
        in_specs=[blk(n_lat, 0), blk(n_lat, H), blk(n_lat, 2 * H), blk(n_ctx, H), blk(n_ctx, 2 * H),
                  gate_spec(n_lat), gate_spec(n_lat), gate_spec(n_ctx), gate_spec(n_ctx),
                  const_spec, const_spec],
        out_specs=pl.BlockSpec((1, hp, L, LANE), lambda b, h: (b, h, 0, 0)),
        scratch_shapes=[pltpu.VMEM((nc, DN_DK, DN_DV), f32), pltpu.VMEM((hp, n_lat, CHUNK, DN_DV), f32),
                        pltpu.VMEM((nc, n_all, CHUNK, DN_DV), f32),
                        pltpu.VMEM((nc, n_all, CHUNK, DN_DK), bf16),
                        pltpu.VMEM((nc, n_all, DN_DK, CHUNK), bf16),
                        pltpu.VMEM((nc, n_all, 8, LANE), f32),
                        pltpu.VMEM((nc, n_lat, CHUNK, DN_DK), bf16),
                        pltpu.VMEM((nc, n_lat, CHUNK, CHUNK), bf16)],
        compiler_params=_params(("arbitrary", "arbitrary")),
    )(qkv_lat, qkv_lat, qkv_lat, qkv_ctx, qkv_ctx, be, ar, bec, arc, alog, dtb)
    return out


def _merge_kernel(x_ref, hm_ref, hd_ref, o_ref_in, z_ref, ga_ref, gb_ref, mlw_ref, dnw_ref, wa_ref, wb_ref,
                  wo_ref, nw1_ref, nw2_ref, g1_ref, sc2_ref, sh2_ref, lat_ref, h2_ref):
    hm = hm_ref[0]
    parts = []
    for h in range(ML_HEADS):
        seg = hm[:, h * ML_DV:(h + 1) * ML_DV]
        ms = jnp.mean(seg * seg, axis=-1, keepdims=True)
        parts.append(seg * lax.rsqrt(ms + EPS))
    ym = jnp.concatenate(parts, axis=-1) * mlw_ref[...] * jax.nn.sigmoid(o_ref_in[0].astype(f32))
    parts = []
    for h in range(DN_HEADS):
        seg = hd_ref[0, h]
        ms = jnp.mean(seg * seg, axis=-1, keepdims=True)
        parts.append(seg * lax.rsqrt(ms + EPS))
    z = z_ref[0].astype(f32)
    yd = jnp.concatenate(parts, axis=-1) * dnw_ref[...] * (z * jax.nn.sigmoid(z))
    ya = jnp.dot(ym.astype(bf16), wa_ref[...], preferred_element_type=f32)
    yb = jnp.dot(yd.astype(bf16), wb_ref[...], preferred_element_type=f32)
    y = jax.nn.sigmoid(ga_ref[0].astype(f32)) * ya + jax.nn.sigmoid(gb_ref[0].astype(f32)) * yb
    ymix = jnp.dot(y.astype(bf16), wo_ref[...], preferred_element_type=f32)
    ms = jnp.mean(ymix * ymix, axis=-1, keepdims=True)
    lat = x_ref[0] + g1_ref[0] * (ymix * lax.rsqrt(ms + EPS) * nw1_ref[...])
    lat_ref[0] = lat
    ms = jnp.mean(lat * lat, axis=-1, keepdims=True)
    h2 = (lat * lax.rsqrt(ms + EPS) * nw2_ref[...]) * (1.0 + sc2_ref[0]) + sh2_ref[0]
    h2_ref[0] = h2.astype(bf16)


def _merge(x, hm, hd, p_lat, mlw, dnw, wa, wb, wo, nw1, nw2, g1, sc2, sh2, *, tt):
    B, L, D = x.shape
    tok = lambda off: pl.BlockSpec((1, tt, D), lambda b, t: (b, t, off))
    full = lambda a: pl.BlockSpec(a.shape, lambda b, t: (0,) * a.ndim)
    per_b = pl.BlockSpec((1, 1, D), lambda b, t: (b, 0, 0))
    return pl.pallas_call(
        _merge_kernel,
        name="merge",
        out_shape=(jax.ShapeDtypeStruct((B, L, D), f32), jax.ShapeDtypeStruct((B, L, D), bf16)),
        grid=(B, L // tt),
        in_specs=[tok(0), tok(0),
                  pl.BlockSpec((1, DN_HEADS, tt, LANE), lambda b, t: (b, 0, t, 0)),
                  tok(2), tok(3), tok(4), tok(5),
                  full(mlw), full(dnw), full(wa), full(wb), full(wo), full(nw1), full(nw2),
                  per_b, per_b, per_b],
        out_specs=(tok(0), tok(0)),
        compiler_params=_params(("arbitrary", "arbitrary")),
    )(x, hm, hd, p_lat, p_lat, p_lat, p_lat, mlw, dnw, wa, wb, wo, nw1, nw2, g1, sc2, sh2)


def _transpose_kernel(x_ref, o_ref):
    o_ref[...] = x_ref[...].T.astype(o_ref.dtype)


def _transposed_bf16(x, *, te):
    E, D = x.shape
    return pl.pallas_call(
        _transpose_kernel,
        name="table_transpose",
        out_shape=jax.ShapeDtypeStruct((D, E), bf16),
        grid=(E // te,),
        in_specs=[pl.BlockSpec((te, D), lambda i: (i, 0))],
        out_specs=pl.BlockSpec((D, te), lambda i: (0, i)),
        compiler_params=_params(("arbitrary",)),
    )(x)


def _cand_pairs():
    return [(i, j) for i in range(PEER_TOPK) for j in range(PEER_TOPK) if (i + 1) * (j + 1) <= PEER_TOPK]


def _sorting_network(n):
    pairs = []
    p = 1
    while p < n:
        k = p
        while k >= 1:
            for j in range(k % p, n - k, 2 * k):
                for i in range(min(k, n - j - k)):
                    if (i + j) // (2 * p) == (i + j + k) // (2 * p):
                        pairs.append((i + j, i + j + k))
            k //= 2
        p *= 2
    return pairs


def _top_sorted(arrays, count):
    lists = []
    for x in arrays:
        n = x.shape[0] // 8
        xr = x.reshape(n, 8, x.shape[1])
        lists.append([xr[v] for v in range(n)])
    for i, j in _sorting_network(len(lists[0])):
        for tiles in lists:
            tiles[i], tiles[j] = jnp.maximum(tiles[i], tiles[j]), jnp.minimum(tiles[i], tiles[j])
    rows = [[] for _ in lists]
    for r in range(count):
        heads = [jnp.max(tiles[0], axis=0, keepdims=True) for tiles in lists]
        for k, m in enumerate(heads):
            rows[k].append(m)
        left = count - 1 - r
        for tiles, m in zip(lists, heads):
            hit = tiles[0] == m
            for k in range(min(left, len(tiles))):
                below = tiles[k + 1] if k + 1 < len(tiles) else NEG_INF
                tiles[k] = jnp.where(hit, below, tiles[k])
    return rows


def _peer_select(h2_ref, wq_ref, keys_ref, rank_ref, f2_ref, cnt_ref, e1_ref, qh_ref, a_ref, b_ref, cand_ref):
    n_tiles = h2_ref.shape[0] // LANE
    width = a_ref.shape[0]
    pairs = _cand_pairs()
    half = PEER_DQ // 2
    top = range(PEER_TOPK)

    def head_body(h, carry):
        qh_ref[...] = jnp.dot(h2_ref[...], wq_ref[h], preferred_element_type=f32)

        def tiles_body(g, carry2):
            lanes = range(width)
            tiles = [g * width + i for i in lanes]
            qt = [qh_ref[pl.ds(pl.multiple_of(t * LANE, LANE), LANE), :].astype(bf16) for t in tiles]
            s1 = [lax.dot_general(keys_ref[h, 0], q[:, :half], NT, preferred_element_type=f32) for q in qt]
            s2 = [lax.dot_general(keys_ref[h, 1], q[:, half:], NT, preferred_element_type=f32) for q in qt]
            tops = _top_sorted([s1[i] for i in lanes] + [s2[i] for i in lanes], PEER_TOPK)
            for i in lanes:
                for r in top:
                    a_ref[i, r:r + 1, :] = tops[i][r]
                    b_ref[i, r:r + 1, :] = tops[width + i][r]
            rank = [jnp.full(s.shape, float(PEER_TOPK), f32) for s in s2]
            for r in reversed(top):
                rank = [jnp.where(s2[i] >= b_ref[i, r:r + 1, :], float(r), rank[i]) for i in lanes]
            cand_ref[...] = jnp.full(cand_ref.shape, NEG_INF, f32)
            for c, (i1, i2) in enumerate(pairs):
                for i in lanes:
                    cand_ref[i, c:c + 1, :] = a_ref[i, i1:i1 + 1, :] + b_ref[i, i2:i2 + 1, :]
            cand = [cand_ref[i] for i in lanes]
            tau = [rows[-1] for rows in _top_sorted(cand, PEER_TOPK)]
            a0 = [a_ref[i, 0:1, :] for i in lanes]
            b0 = [b_ref[i, 0:1, :] for i in lanes]
            z = [jnp.sum(jnp.where(cand[i] >= tau[i], jnp.exp(cand[i] - (a0[i] + b0[i])), 0.0), axis=0,
                         keepdims=True) for i in lanes]
            cnt = [jnp.zeros(s.shape, f32) for s in s1]
            for r in top:
                cnt = [jnp.where(s1[i] + b_ref[i, r:r + 1, :] >= tau[i], float(r + 1), cnt[i]) for i in lanes]
            for i in lanes:
                rank_ref[h, tiles[i]] = rank[i].astype(bf16)
                f2_ref[h, tiles[i]] = (jnp.exp(s2[i] - b0[i]) * (0.5 / z[i])).astype(bf16)
                cnt_ref[h, tiles[i]] = cnt[i]
                e1_ref[h, tiles[i]] = jnp.exp(s1[i] - a0[i])
            return carry2

        return lax.fori_loop(0, n_tiles // width, tiles_body, carry)

    lax.fori_loop(0, PEER_HEADS, head_body, 0)


def _peer_kernel(h2_ref, lat_ref, wq_ref, keys_ref, u_ref, vt_ref, g2_ref, nw_ref, o_ref,
                 rank_ref, f2_ref, cnt_ref, e1_ref, qh_ref, a_ref, b_ref, cand_ref, wg_ref, yt_ref):
    j = pl.program_id(1)
    n_blk = u_ref.shape[0] // PEER_NKEYS
    n_tiles = h2_ref.shape[0] // LANE
    n_parts = 2
    per = n_blk // n_parts

    @pl.when(j == 0)
    def _():
        _peer_select(h2_ref, wq_ref, keys_ref, rank_ref, f2_ref, cnt_ref, e1_ref, qh_ref, a_ref, b_ref, cand_ref)
        yt_ref[...] = jnp.zeros_like(yt_ref)

    def gate_group(g, carry):
        t = g // n_parts
        p = g % n_parts
        ws = [None] * per
        for h in range(PEER_HEADS):
            rk = rank_ref[h, t]
            f2 = f2_ref[h, t]
            for b in range(per):
                i1 = j * n_blk + p * per + b
                cn = cnt_ref[h, t, pl.ds(i1, 1), :].astype(bf16)
                ee = e1_ref[h, t, pl.ds(i1, 1), :].astype(bf16)
                term = jnp.where(rk < cn, f2, 0) * ee
                ws[b] = term if ws[b] is None else ws[b] + term
        for b in range(per):
            r0 = pl.multiple_of((p * per + b) * PEER_NKEYS, PEER_NKEYS)
            wg_ref[t, pl.ds(r0, PEER_NKEYS), :] = ws[b]
        return carry

    lax.fori_loop(0, n_tiles * n_parts, gate_group, 0)

    at = lax.dot_general(u_ref[...], h2_ref[...], NT, preferred_element_type=f32)
    act = (at * (1.0 + lax.erf(at * (2.0 ** -0.5)))).astype(bf16)
    wg = jnp.concatenate([wg_ref[t] * act[:, t * LANE:(t + 1) * LANE] for t in range(n_tiles)], axis=1)
    yt_ref[...] += jnp.dot(vt_ref[...], wg, preferred_element_type=f32)

    @pl.when(j == pl.num_programs(1) - 1)
    def _():
        y = yt_ref[...].T
        ms = jnp.mean(y * y, axis=-1, keepdims=True)
        o_ref[...] = lat_ref[...] + g2_ref[0] * (y * lax.rsqrt(ms + EPS) * nw_ref[...])


def _peer(h2, lat, wq, keys, u, vt, g2, nw, *, tokens_per_batch, tt, ec):
    N, D = h2.shape
    E = u.shape[0]
    tiles_per_batch = tokens_per_batch // tt
    sel_width = 2 if (tt // LANE) % 2 == 0 else 1
    sel = lambda dt: pltpu.VMEM((PEER_HEADS, tt // LANE, PEER_NKEYS, LANE), dt)
    return pl.pallas_call(
        _peer_kernel,
        name="peer",
        out_shape=jax.ShapeDtypeStruct((N, D), f32),
        grid=(N // tt, E // ec),
        in_specs=[pl.BlockSpec((tt, D), lambda i, j: (i, 0)),
                  pl.BlockSpec((tt, D), lambda i, j: (i, 0)),
                  pl.BlockSpec(wq.shape, lambda i, j: (0, 0, 0)),
                  pl.BlockSpec(keys.shape, lambda i, j: (0, 0, 0, 0)),
                  pl.BlockSpec((ec, D), lambda i, j: (j, 0)),
                  pl.BlockSpec((D, ec), lambda i, j: (0, j)),
                  pl.BlockSpec((1, 1, D), lambda i, j: (i // tiles_per_batch, 0, 0)),
                  pl.BlockSpec((1, D), lambda i, j: (0, 0))],
        out_specs=pl.BlockSpec((tt, D), lambda i, j: (i, 0)),
        scratch_shapes=[sel(bf16), sel(bf16), sel(f32), sel(f32),
                        pltpu.VMEM((tt, PEER_DQ), f32),
                        pltpu.VMEM((sel_width, PEER_TOPK, LANE), f32),
                        pltpu.VMEM((sel_width, PEER_TOPK, LANE), f32),
                        pltpu.VMEM((sel_width, CHUNK, LANE), f32),
                        pltpu.VMEM((tt // LANE, ec, LANE), bf16),
                        pltpu.VMEM((D, tt), f32)],
        compiler_params=_params(("arbitrary", "arbitrary")),
    )(h2, lat, wq, keys, u, vt, g2, nw)


def _dir_rows(t, n_heads, width=CHUNK):
    B, L, _ = t.shape
    return t.reshape(B, L, N_DIR, n_heads).transpose(0, 2, 3, 1).reshape(B, N_DIR, n_heads, L // width, width)


def _to_colmajor(t, rows):
    B, _, C = t.shape
    return t.reshape(B, rows, GRID_W, C).transpose(0, 2, 1, 3).reshape(B, GRID_W * rows, C)


def _pick(n, prefs):
    for p in prefs:
        if n % p == 0:
            return p
    raise ValueError(f"no tile in {prefs} divides {n}")


def kernel(x, c, ctx, c_ctx, w_mod, b_mod, norm_w, w_in, ml_gate_bias, dn_a_log, dn_dt_bias, dn_conv_w,
           ml_norm_w, dn_norm_w, w_branch_a, w_branch_b, w_out, peer_wq, peer_keys, peer_u, peer_v):
    B, L, D = x.shape
    Lc = ctx.shape[1]
    assert w_mod.shape[0] == 1, "single-layer block"
    assert L % GRID_W == 0 and L % CHUNK == 0 and Lc % CHUNK == 0
    rows = L // GRID_W
    assert rows % 8 == 0 and CHUNK % rows == 0
    assert D == ML_HEADS * ML_DV == DN_HEADS * DN_DV

    mod = _modulation(jnp.concatenate([c, c_ctx[None, :]], axis=0), w_mod[0], b_mod[0])
    mod = mod.reshape(B + 1, N_MOD, 1, D)
    sh1, sc1, g1, sh2, sc2, g2 = (mod[:B, i] for i in range(N_MOD))
    sh1c, sc1c = mod[B:, 0], mod[B:, 1]

    w = w_in[0]
    sizes = (512, 512, 1024, 1024, 8, 8, 1024, 1024, 1024, 1024, 16, 16, 1024, 1024)
    offs = [0]
    for s in sizes:
        offs.append(offs[-1] + s)
    col = lambda i: w[:, offs[i]:offs[i + 1]]
    w_main = jnp.concatenate([col(0), col(1), col(2), col(3), col(9), col(12), col(13)], axis=1).astype(bf16)
    w_dn = jnp.concatenate([col(6), col(7), col(8)], axis=1).astype(bf16)
    w_gate = jnp.concatenate([col(4), col(5), col(10), col(11)], axis=1)
    n_gate = w_gate.shape[1]
    w_gate = jnp.pad(w_gate, ((0, 0), (0, LANE - n_gate))).astype(bf16)

    nw = norm_w[0]
    tt_lat = _pick(L, (1024, 512, 256, 128, 64))
    tt_ctx = _pick(Lc, (256, 128, 64))
    proj = lambda xs, sc_, sh_, tt: (
        _project(xs, nw[0:1], sc_, sh_, w_main, tn=1024, tt=tt, head_major=False, out_dtype=bf16),
        _project(xs, nw[0:1], sc_, sh_, w_dn, tn=1024, tt=tt, head_major=True, out_dtype=f32),
        _project(xs, nw[0:1], sc_, sh_, w_gate, tn=LANE, tt=tt, head_major=False, out_dtype=f32))
    p_lat, dn_lat, g_lat = proj(x, sc1, sh1, tt_lat)
    p_ctx, dn_ctx, g_ctx = proj(ctx, sc1c, sh1c, tt_ctx)

    nh = N_DIR * ML_HEADS
    nd = N_DIR * DN_HEADS
    gi, gf = _dir_rows(g_lat[..., 0:nh], ML_HEADS), _dir_rows(g_lat[..., nh:2 * nh], ML_HEADS)
    gic, gfc = _dir_rows(g_ctx[..., 0:nh], ML_HEADS), _dir_rows(g_ctx[..., nh:2 * nh], ML_HEADS)
    dn_width = CHUNK * math.gcd(math.gcd(L // CHUNK, Lc // CHUNK), 4)
    be = _dir_rows(_to_colmajor(g_lat[..., 2 * nh:2 * nh + nd], rows), DN_HEADS, dn_width)
    ar = _dir_rows(_to_colmajor(g_lat[..., 2 * nh + nd:2 * nh + 2 * nd], rows), DN_HEADS, dn_width)
    bec = _dir_rows(g_ctx[..., 2 * nh:2 * nh + nd], DN_HEADS, dn_width)
    arc = _dir_rows(g_ctx[..., 2 * nh + nd:2 * nh + 2 * nd], DN_HEADS, dn_width)
    bias = jnp.broadcast_to(ml_gate_bias[0].astype(f32)[:, :, :, None, None], (2, N_DIR, ML_HEADS, 1, CHUNK))
    alog = jnp.broadcast_to(dn_a_log[0].astype(f32)[:, :, None, None], (N_DIR, DN_HEADS, 1, dn_width))
    dtb = jnp.broadcast_to(dn_dt_bias[0].astype(f32)[:, :, None, None], (N_DIR, DN_HEADS, 1, dn_width))

    hm = _mlstm(p_lat, p_ctx, gi, gf, gic, gfc, bias)

    taps = dn_conv_w[0].astype(f32).reshape(DN_CONV, 3 * DN_HEADS, LANE).transpose(1, 0, 2)
    taps = jnp.pad(taps, ((0, 0), (0, 8 - DN_CONV), (0, 0)))
    qkv_lat = _dn_prep(dn_lat, taps, seq_len=rows, n_cols=GRID_W, heads_per_step=2)
    qkv_ctx = _dn_prep(dn_ctx, taps, seq_len=Lc, n_cols=1, heads_per_step=DN_HEADS)
    hd = _gdn(qkv_lat, qkv_ctx, be, ar, bec, arc, alog, dtb, rows=rows)

    lat1, h2 = _merge(x, hm, hd, p_lat, ml_norm_w[0].reshape(1, D).astype(f32),
                      jnp.tile(dn_norm_w[0].astype(f32), DN_HEADS).reshape(1, D),
                      w_branch_a[0].astype(bf16), w_branch_b[0].astype(bf16), w_out[0].astype(bf16),
                      nw[1:2], nw[2:3], g1, sc2, sh2, tt=_pick(L, (256, 128, 64)))

    wq = peer_wq[0].astype(bf16).reshape(D, PEER_HEADS, PEER_DQ).transpose(1, 0, 2)
    out = _peer(h2.reshape(B * L, D), lat1.reshape(B * L, D), wq, peer_keys[0].astype(bf16),
                peer_u[0].astype(bf16), _transposed_bf16(peer_v[0], te=1024), g2, nw[3:4],
                tokens_per_batch=L, tt=_pick(L, (512, 256)), ec=2048)
    return out.reshape(B, L, D)
```

```python
import functools
import math

import jax
import jax.numpy as jnp
from jax import lax
from jax.experimental import pallas as pl
from jax.experimental.pallas import tpu as pltpu

f32 = jnp.float32
bf16 = jnp.bfloat16

EPS = 1e-6
GRID_W = 64
N_DIR = 2
N_MOD = 6
ML_HEADS, ML_DQK, ML_DV = 4, 128, 256
DN_HEADS, DN_DK, DN_DV, DN_CONV = 8, 128, 128, 5
CHUNK = 64
PEER_HEADS, PEER_NKEYS, PEER_DQ, PEER_TOPK = 8, 128, 256, 16
LANE = 128
VMEM_LIMIT = 56 * 1024 * 1024

NT = (((1,), (1,)), ((), ()))
TN = (((0,), (0,)), ((), ()))
NEG_INF = float("-inf")


def _params(sem):
    return pltpu.CompilerParams(dimension_semantics=sem, vmem_limit_bytes=VMEM_LIMIT)


def _mod_kernel(c_ref, w_ref, b_ref, o_ref):
    c = c_ref[...]
    s = c * jax.nn.sigmoid(c)
    o_ref[...] = jnp.dot(s, w_ref[...], preferred_element_type=f32) + b_ref[...]


def _modulation(cond, w_mod, b_mod):
    n, d = cond.shape
    return pl.pallas_call(
        _mod_kernel,
        name="modulation",
        out_shape=jax.ShapeDtypeStruct((n, N_MOD * d), f32),
        grid=(N_MOD,),
        in_specs=[pl.BlockSpec((n, d), lambda j: (0, 0)),
                  pl.BlockSpec((d, d), lambda j: (0, j)),
                  pl.BlockSpec((1, d), lambda j: (0, j))],
        out_specs=pl.BlockSpec((n, d), lambda j: (0, j)),
        compiler_params=_params(("arbitrary",)),
    )(cond, w_mod, b_mod.reshape(1, -1))


def _proj_kernel(x_ref, nw_ref, sc_ref, sh_ref, w_ref, o_ref, h_ref, *, head_major):
    @pl.when(pl.program_id(2) == 0)
    def _():
        x = x_ref[0]
        ms = jnp.mean(x * x, axis=-1, keepdims=True)
        h = (x * lax.rsqrt(ms + EPS)) * nw_ref[...]
        h_ref[...] = (h * (1.0 + sc_ref[0]) + sh_ref[0]).astype(bf16)

    acc = jnp.dot(h_ref[...], w_ref[...], preferred_element_type=f32).astype(o_ref.dtype)
    if head_major:
        for i in range(acc.shape[1] // LANE):
            o_ref[0, i] = acc[:, i * LANE:(i + 1) * LANE]
    else:
        o_ref[0] = acc


def _project(xs, nw, sc, sh, w, *, tn, tt, head_major, out_dtype):
    B, L, D = xs.shape
    N = w.shape[1]
    per_batch = sc.shape[0] == B and B > 1
    mod_map = (lambda b, t, j: (b, 0, 0)) if per_batch else (lambda b, t, j: (0, 0, 0))
    if head_major:
        out_shape = jax.ShapeDtypeStruct((B, N // LANE, L, LANE), out_dtype)
        out_spec = pl.BlockSpec((1, tn // LANE, tt, LANE), lambda b, t, j: (b, j, t, 0))
    else:
        out_shape = jax.ShapeDtypeStruct((B, L, N), out_dtype)
        out_spec = pl.BlockSpec((1, tt, tn), lambda b, t, j: (b, t, j))
    return pl.pallas_call(
        functools.partial(_proj_kernel, head_major=head_major),
        name="in_proj",
        out_shape=out_shape,
        grid=(B, L // tt, N // tn),
        in_specs=[pl.BlockSpec((1, tt, D), lambda b, t, j: (b, t, 0)),
                  pl.BlockSpec((1, D), lambda b, t, j: (0, 0)),
                  pl.BlockSpec((1, 1, D), mod_map),
                  pl.BlockSpec((1, 1, D), mod_map),
                  pl.BlockSpec((D, tn), lambda b, t, j: (0, j))],
        out_specs=out_spec,
        scratch_shapes=[pltpu.VMEM((tt, D), bf16)],
        compiler_params=_params(("arbitrary", "arbitrary", "arbitrary")),
    )(xs, nw, sc, sh, w)


def _chunk_masks(reverse, n=1):
    size = n * CHUNK
    row = lax.broadcasted_iota(jnp.int32, (size, size), 0)
    col = lax.broadcasted_iota(jnp.int32, (size, size), 1)
    eye = row == col
    same = (row // CHUNK) == (col // CHUNK) if n > 1 else None
    both = (lambda m: jnp.logical_and(same, m)) if n > 1 else (lambda m: m)
    if reverse:
        return eye, both(col >= row), both(row >= col), same
    return eye, both(col <= row), both(row <= col), same


def _row_to_col(v_row, eye):
    return jnp.sum(jnp.where(eye, jnp.broadcast_to(v_row, eye.shape), 0.0), axis=1, keepdims=True)


def _cumsum_forms(v_row, eye, incl, incl_t):
    vb = jnp.broadcast_to(v_row, eye.shape)
    c_col = jnp.sum(jnp.where(incl, vb, 0.0), axis=1, keepdims=True)
    v_col = jnp.sum(jnp.where(eye, vb, 0.0), axis=1, keepdims=True)
    c_row = jnp.sum(jnp.where(incl_t, v_col, 0.0), axis=0, keepdims=True)
    return c_col, c_row


def _ml_chunks(problems, c_ref, n_ref, want_out):
    dirs = range(len(problems))
    gate = []
    for qs, k, v, ig_row, lf_row, m, masks in problems:
        eye, incl, incl_t, _ = masks
        b_col, b_row = _cumsum_forms(lf_row, eye, incl, incl_t)
        b_last = jnp.sum(lf_row, axis=1, keepdims=True)
        g_row = b_last - b_row + ig_row
        m_chunk = jnp.max(g_row, axis=1, keepdims=True)
        e_col = _row_to_col(jnp.exp(g_row - m_chunk), eye)
        ek = k * e_col
        gate.append((b_col, b_row, b_last, m_chunk, ek))
    v16 = [p[2].astype(bf16) for p in problems]
    kv = [lax.dot_general(gate[d][4].astype(bf16), v16[d], TN, preferred_element_type=f32) for d in dirs]
    c_old = [c_ref[d] for d in dirs]
    n_old = [n_ref[d] for d in dirs]
    hs = [None for _ in dirs]
    if want_out:
        q16 = [p[0].astype(bf16) for p in problems]
        qk = [lax.dot_general(q16[d], problems[d][1].astype(bf16), NT, preferred_element_type=f32) for d in dirs]
        qc = [jnp.dot(q16[d], c_old[d].astype(bf16), preferred_element_type=f32) for d in dirs]
        w_in, m_in = [], []
        for d in dirs:
            b_col, b_row = gate[d][0], gate[d][1]
            dlog = jnp.where(problems[d][6][1], b_col - b_row + problems[d][3], NEG_INF)
            mi = jnp.max(dlog, axis=1, keepdims=True)
            m_in.append(mi)
            w_in.append(jnp.exp(dlog - mi) * qk[d])
        num_in = [jnp.dot(w_in[d].astype(bf16), v16[d], preferred_element_type=f32) for d in dirs]
        for d in dirs:
            qs, m = problems[d][0], problems[d][5]
            den_in = jnp.sum(w_in[d], axis=1, keepdims=True)
            m_inter = gate[d][0] + m
            m_t = jnp.maximum(m_inter, m_in[d])
            a = jnp.exp(m_inter - m_t)
            r = jnp.exp(m_in[d] - m_t)
            num = a * qc[d] + r * num_in[d]
            den = a * jnp.sum(qs * n_old[d], axis=1, keepdims=True) + r * den_in
            hs[d] = num / jnp.maximum(jnp.abs(den), jnp.exp(-m_t))
    ms = []
    for d in dirs:
        _, _, b_last, m_chunk, ek = gate[d]
        m = problems[d][5]
        m_new = jnp.maximum(b_last + m, m_chunk)
        sp = jnp.exp(b_last + m - m_new)
        sc = jnp.exp(m_chunk - m_new)
        c_ref[d] = sp * c_old[d] + sc * kv[d]
        n_ref[d] = sp * n_old[d] + sc * jnp.sum(ek, axis=0, keepdims=True)
        ms.append(m_new)
    return hs, ms


def _mlstm_kernel(q_ref, k_ref, v_ref, kc_ref, vc_ref, gi_ref, gf_ref, gic_ref, gfc_ref, bias_ref,
                  o_ref, c_ref, n_ref):
    n_lat = q_ref.shape[1] // CHUNK
    n_ctx = kc_ref.shape[1] // CHUNK
    scale = ML_DQK ** -0.5
    c_ref[...] = jnp.zeros_like(c_ref)
    n_ref[...] = jnp.zeros_like(n_ref)
    o_ref[...] = jnp.zeros_like(o_ref)
    masks = [_chunk_masks(False), _chunk_masks(True)]

    n_heads = q_ref.shape[2] // ML_DQK
    chains = [(hh, d) for hh in range(n_heads) for d in range(N_DIR)]
    qk_cols = lambda hh: slice(hh * ML_DQK, (hh + 1) * ML_DQK)
    v_cols = lambda hh: slice(hh * ML_DV, (hh + 1) * ML_DV)

    def gates(i_ref, f_ref, hh, d, c):
        ig = i_ref[0, d, hh, pl.ds(c, 1), :] + bias_ref[0, d, hh]
        lf = jax.nn.log_sigmoid(f_ref[0, d, hh, pl.ds(c, 1), :] + bias_ref[1, d, hh])
        return ig, lf

    def ctx_body(i, ms):
        problems = []
        for n, (hh, d) in enumerate(chains):
            c = i if d == 0 else n_ctx - 1 - i
            sl = pl.ds(pl.multiple_of(c * CHUNK, CHUNK), CHUNK)
            ig, lf = gates(gic_ref, gfc_ref, hh, d, c)
            problems.append((None, kc_ref[0, sl, qk_cols(hh)].astype(f32), vc_ref[0, sl, v_cols(hh)], ig, lf,
                             ms[n], masks[d]))
        _, out = _ml_chunks(problems, c_ref, n_ref, False)
        return tuple(out)

    def lat_body(i, ms):
        problems, slices = [], []
        for n, (hh, d) in enumerate(chains):
            c = i if d == 0 else n_lat - 1 - i
            sl = pl.ds(pl.multiple_of(c * CHUNK, CHUNK), CHUNK)
            ig, lf = gates(gi_ref, gf_ref, hh, d, c)
            problems.append((q_ref[0, sl, qk_cols(hh)].astype(f32) * scale, k_ref[0, sl, qk_cols(hh)].astype(f32),
                             v_ref[0, sl, v_cols(hh)], ig, lf, ms[n], masks[d]))
            slices.append(sl)
        hs, out = _ml_chunks(problems, c_ref, n_ref, True)
        for n, (hh, d) in enumerate(chains):
            o_ref[0, slices[n], v_cols(hh)] += hs[n]
        return tuple(out)

    m0 = tuple(jnp.zeros((1, 1), f32) for _ in chains)
    ms = lax.fori_loop(0, n_ctx, ctx_body, m0)
    lax.fori_loop(0, n_lat, lat_body, ms, unroll=2)


def _mlstm(p_lat, p_ctx, gi, gf, gic, gfc, bias):
    B, L, _ = p_lat.shape
    Lc = p_ctx.shape[1]
    n_lat, n_ctx = L // CHUNK, Lc // CHUNK
    hp = ML_HEADS
    qk_w, v_w = hp * ML_DQK, hp * ML_DV
    k_off = ML_HEADS * ML_DQK // qk_w
    v_off = 2 * ML_HEADS * ML_DQK // v_w
    gate_spec = lambda n: pl.BlockSpec((1, N_DIR, hp, n, CHUNK), lambda b, h: (b, 0, h, 0, 0))
    return pl.pallas_call(
        _mlstm_kernel,
        name="mlstm_scan",
        out_shape=jax.ShapeDtypeStruct((B, L, ML_HEADS * ML_DV), f32),
        grid=(B, ML_HEADS // hp),
        in_specs=[pl.BlockSpec((1, L, qk_w), lambda b, h: (b, 0, h)),
                  pl.BlockSpec((1, L, qk_w), lambda b, h: (b, 0, k_off + h)),
                  pl.BlockSpec((1, L, v_w), lambda b, h: (b, 0, v_off + h)),
                  pl.BlockSpec((1, Lc, qk_w), lambda b, h: (b, 0, k_off + h)),
                  pl.BlockSpec((1, Lc, v_w), lambda b, h: (b, 0, v_off + h)),
                  gate_spec(n_lat), gate_spec(n_lat), gate_spec(n_ctx), gate_spec(n_ctx),
                  pl.BlockSpec((2, N_DIR, hp, 1, CHUNK), lambda b, h: (0, 0, h, 0, 0))],
        out_specs=pl.BlockSpec((1, L, v_w), lambda b, h: (b, 0, h)),
        scratch_shapes=[pltpu.VMEM((hp * N_DIR, ML_DQK, ML_DV), f32), pltpu.VMEM((hp * N_DIR, 1, ML_DQK), f32)],
        compiler_params=_params(("arbitrary", "arbitrary")),
    )(p_lat, p_lat, p_lat, p_ctx, p_ctx, gi, gf, gic, gfc, bias)


def _dn_prep_kernel(x_ref, w_ref, o_ref, pad_ref, *, seq_len, n_cols):
    nh = x_ref.shape[1]
    kind = (pl.program_id(1) * nh) // DN_HEADS
    half = DN_CONV // 2
    total = seq_len * n_cols
    pad = (pad_ref.shape[0] - total) // 2
    pad_ref[0:pad, :] = jnp.zeros((pad, LANE), f32)
    pad_ref[pad + total:2 * pad + total, :] = jnp.zeros((pad, LANE), f32)
    q_scale = jnp.where(kind == 0, DN_DK ** -0.5, 1.0).astype(f32)
    for hh in range(nh):
        pad_ref[pad:pad + total, :] = x_ref[0, hh]
        y = jnp.zeros((total, LANE), f32)
        for t in range(DN_CONV):
            y = y + pad_ref[pl.ds(pad + (t - half) * n_cols, total), :] * w_ref[hh, t:t + 1, :]
        y = y * jax.nn.sigmoid(y)
        ss = jnp.sum(y * y, axis=-1, keepdims=True)
        y = y * jnp.where(kind == 2, 1.0, lax.rsqrt(ss + EPS) * q_scale)
        if n_cols > 1:
            y = pltpu.einshape("rcd->crd", y.reshape(seq_len, n_cols, LANE))
        o_ref[0, hh] = y.reshape(o_ref.shape[2:]).astype(o_ref.dtype)


def _dn_prep(raw, wt, *, seq_len, n_cols, heads_per_step):
    B, G, L, _ = raw.shape
    n_chunks = L // CHUNK
    nh = heads_per_step
    assert DN_HEADS % nh == 0
    pad = -(-(DN_CONV // 2) * n_cols // 8) * 8
    return pl.pallas_call(
        functools.partial(_dn_prep_kernel, seq_len=seq_len, n_cols=n_cols),
        name="gdn_prep",
        out_shape=jax.ShapeDtypeStruct((B, G, n_chunks, CHUNK, LANE), bf16),
        grid=(B, G // nh),
        in_specs=[pl.BlockSpec((1, nh, L, LANE), lambda b, g: (b, g, 0, 0)),
                  pl.BlockSpec((nh, 8, LANE), lambda b, g: (g, 0, 0))],
        out_specs=pl.BlockSpec((1, nh, n_chunks, CHUNK, LANE), lambda b, g: (b, g, 0, 0, 0)),
        scratch_shapes=[pltpu.VMEM((L + 2 * pad, LANE), f32)],
        compiler_params=_params(("arbitrary", "arbitrary")),
    )(raw, wt)


def _mm16(a, b):
    return jnp.dot(a.astype(bf16), b.astype(bf16), preferred_element_type=f32)


def _unit_tri_inverses(mats, eye):
    row = lax.broadcasted_iota(jnp.int32, eye.shape, 0)
    col = lax.broadcasted_iota(jnp.int32, eye.shape, 1)
    same = lambda bits: (row >> bits) == (col >> bits)
    m8 = [jnp.where(same(3), m, 0.0) for m in mats]
    p2 = [_mm16(m, m) for m in m8]
    p4 = [_mm16(p, p) for p in p2]
    inv = [eye.astype(f32) - m for m in m8]
    inv = [i + _mm16(i, p) for i, p in zip(inv, p2)]
    inv = [i + _mm16(i, p) for i, p in zip(inv, p4)]
    for bits in (3, 4, 5):
        joining = jnp.logical_and(same(bits + 1), jnp.logical_not(same(bits)))
        right = [_mm16(jnp.where(joining, m, 0.0), i) for m, i in zip(mats, inv)]
        inv = [i - _mm16(i, r) for i, r in zip(inv, right)]
    return inv


def _dn_prepare(problems):
    n = len(problems)
    pre = []
    for q, k, v, braw_row, araw_row, a_scale, dt_bias, masks in problems:
        eye, incl, incl_t, same = masks
        beta_col = _row_to_col(jax.nn.sigmoid(braw_row), eye)
        g_row = a_scale * jax.nn.softplus(araw_row + dt_bias)
        gb = jnp.broadcast_to(g_row, eye.shape)
        gc_col = jnp.sum(jnp.where(incl, gb, 0.0), axis=1, keepdims=True)
        g_col = jnp.sum(jnp.where(eye, gb, 0.0), axis=1, keepdims=True)
        gc_row = jnp.sum(jnp.where(incl_t, g_col, 0.0), axis=0, keepdims=True)
        if same is None:
            g_last = jnp.sum(g_row, axis=1, keepdims=True)
        else:
            g_last = jnp.sum(jnp.where(same, gb, 0.0), axis=1, keepdims=True)
        gam = jnp.exp(jnp.where(incl, gc_col - gc_row, NEG_INF))
        pre.append((beta_col, gc_col, g_last, gam, k * beta_col, k.astype(bf16)))
    eye = problems[0][7][0]
    gram = [lax.dot_general(pre[i][4].astype(bf16), pre[i][5], NT, preferred_element_type=f32) for i in range(n)]
    mats = []
    for i in range(n):
        _, incl, _, _ = problems[i][7]
        mats.append(jnp.where(jnp.logical_and(incl, jnp.logical_not(eye)), gram[i] * pre[i][3], 0.0))
    inv = [m.astype(bf16) for m in _unit_tri_inverses(mats, eye)]
    egc = [jnp.exp(pre[i][1]) for i in range(n)]
    rhs = [jnp.concatenate([problems[i][2] * pre[i][0], pre[i][4] * egc[i]], axis=1).astype(bf16) for i in range(n)]
    uw = [jnp.dot(inv[i], rhs[i], preferred_element_type=f32) for i in range(n)]
    u = [x[:, :DN_DV] for x in uw]
    w = [x[:, DN_DV:] for x in uw]
    a_qk = [None if problems[i][0] is None else
            lax.dot_general(problems[i][0].astype(bf16), pre[i][5], NT, preferred_element_type=f32) * pre[i][3]
            for i in range(n)]
    out = []
    for i in range(n):
        q, k = problems[i][0], problems[i][1]
        res = [u[i], w[i].astype(bf16), k * jnp.exp(pre[i][2] - pre[i][1]), jnp.exp(pre[i][2])]
        if q is not None:
            res += [(q * egc[i]).astype(bf16), a_qk[i].astype(bf16)]
        out.append(res)
    return out


def _gdn_kernel(q_ref, k_ref, v_ref, kc_ref, vc_ref, be_ref, ar_ref, bec_ref, arc_ref, alog_ref, dtb_ref,
                o_ref, s_ref, acc_ref, u_ref, w_ref, kd_ref, dl_ref, qg_ref, aq_ref, *, rows, group):
    n_lat = q_ref.shape[2]
    n_ctx = kc_ref.shape[2]
    size = group * CHUNK
    masks = [_chunk_masks(False, group), _chunk_masks(True, group)]

    n_heads = k_ref.shape[1]
    chains = [(hh, d) for hh in range(n_heads) for d in range(N_DIR)]

    def prepare(i, base, qr, kr, vr, ber, arr):
        problems = []
        blk = pl.ds(i * group, group)
        for hh, d in chains:
            q = None if qr is None else qr[0, hh, blk].reshape(size, DN_DK).astype(f32)
            k = kr[0, hh, blk].reshape(size, DN_DK).astype(f32)
            v = vr[0, hh, blk].reshape(size, DN_DV).astype(f32)
            problems.append((q, k, v, ber[0, d, hh, pl.ds(i, 1), :], arr[0, d, hh, pl.ds(i, 1), :],
                             -jnp.exp(alog_ref[d, hh]), dtb_ref[d, hh], masks[d]))
        results = _dn_prepare(problems)
        for n, res in enumerate(results):
            dst = pl.ds(base + i * group, group)
            u_ref[n, dst] = res[0].reshape(group, CHUNK, DN_DV)
            w_ref[n, dst] = res[1].reshape(group, CHUNK, DN_DK)
            dl = jnp.broadcast_to(res[3], (size, LANE))
            for g in range(group):
                c = base + i * group + g
                kd_ref[n, c] = res[2][g * CHUNK:(g + 1) * CHUNK].T.astype(bf16)
                dl_ref[n, c] = dl[g * CHUNK:g * CHUNK + 8]
            if qr is not None:
                qg_ref[n, pl.ds(i * group, group)] = res[4].reshape(group, CHUNK, DN_DK)
                for g in range(group):
                    aq_ref[n, i * group + g] = res[5][g * CHUNK:(g + 1) * CHUNK, g * CHUNK:(g + 1) * CHUNK]

    def ctx_prep(i, carry):
        prepare(i, 0, None, kc_ref, vc_ref, bec_ref, arc_ref)
        return carry

    def lat_prep(i, carry):
        prepare(i, n_ctx, q_ref, k_ref, v_ref, be_ref, ar_ref)
        return carry

    lax.fori_loop(0, n_ctx // group, ctx_prep, 0)
    lax.fori_loop(0, n_lat // group, lat_prep, 0, unroll=4)

    s_ref[...] = jnp.zeros_like(s_ref)
    acc_ref[...] = jnp.zeros_like(acc_ref)

    def step(cs, cls):
        idx = range(len(chains))
        s_old = [s_ref[n] for n in idx]
        s16 = [s.astype(bf16) for s in s_old]
        ws = [jnp.dot(w_ref[n, cs[chains[n][1]]], s16[n], preferred_element_type=f32) for n in idx]
        vn16 = [(u_ref[n, cs[chains[n][1]]] - ws[n]).astype(bf16) for n in idx]
        upd = [jnp.dot(kd_ref[n, cs[chains[n][1]]], vn16[n], preferred_element_type=f32) for n in idx]
        for n in idx:
            s_ref[n] = dl_ref[n, cs[chains[n][1]]][0:1, :] * s_old[n] + upd[n]
        if cls is not None:
            inter = [jnp.dot(qg_ref[n, cls[chains[n][1]]], s16[n], preferred_element_type=f32) for n in idx]
            intra = [jnp.dot(aq_ref[n, cls[chains[n][1]]], vn16[n], preferred_element_type=f32) for n in idx]
            for n in idx:
                hh, d = chains[n]
                acc_ref[hh, cls[d]] += inter[n] + intra[n]

    def ctx_step(i, carry):
        step((i, n_ctx - 1 - i), None)
        return carry

    def lat_step(i, carry):
        step((n_ctx + i, n_ctx + n_lat - 1 - i), (i, n_lat - 1 - i))
        return carry

    lax.fori_loop(0, n_ctx, ctx_step, 0, unroll=True)
    lax.fori_loop(0, n_lat, lat_step, 0, unroll=4)
    for hh in range(n_heads):
        o = acc_ref[hh].reshape(GRID_W, rows, DN_DV)
        o_ref[0, hh] = pltpu.einshape("crd->rcd", o).reshape(rows * GRID_W, DN_DV)


def _gdn(qkv_lat, qkv_ctx, be, ar, bec, arc, alog, dtb, *, rows):
    B, _, n_lat, _, _ = qkv_lat.shape
    n_ctx = qkv_ctx.shape[2]
    L = n_lat * CHUNK
    H = DN_HEADS
    n_all = n_ctx + n_lat
    width = be.shape[-1]
    group = width // CHUNK
    hp = 2
    nc = hp * N_DIR
    blk = lambda n, off: pl.BlockSpec((1, hp, n, CHUNK, LANE), lambda b, h: (b, off // hp + h, 0, 0, 0))
    gate_spec = lambda n: pl.BlockSpec((1, N_DIR, hp, n // group, width), lambda b, h: (b, 0, h, 0, 0))
    const_spec = pl.BlockSpec((N_DIR, hp, 1, width), lambda b, h: (0, h, 0, 0))
    out = pl.pallas_call(
        functools.partial(_gdn_kernel, rows=rows, group=group),
        name="gdn_scan",
        out_shape=jax.ShapeDtypeStruct((B, H, L, LANE), f32),
        grid=(B, H // hp),
        in_specs=[blk(n_lat, 0), blk(n_lat, H), blk(n_lat, 2 * H), blk(n_ctx, H), blk(n_ctx, 2 * H),
                  gate_spec(n_lat), gate_spec(n_lat), gate_spec(n_ctx), gate_spec(n_ctx),
                  const_spec, const_spec],
        out_specs=pl.BlockSpec((1, hp, L, LANE), lambda b, h: (b, h, 0, 0)),
        scratch_shapes=[pltpu.VMEM((nc, DN_DK, DN_DV), f32), pltpu.VMEM((hp, n_lat, CHUNK, DN_DV), f32),
                        pltpu.VMEM((nc, n_all, CHUNK, DN_DV), f32),
                        pltpu.VMEM((nc, n_all, CHUNK, DN_DK), bf16),
                        pltpu.VMEM((nc, n_all, DN_DK, CHUNK), bf16),
                        pltpu.VMEM((nc, n_all, 8, LANE), f32),
                        pltpu.VMEM((nc, n_lat, CHUNK, DN_DK), bf16),
                        pltpu.VMEM((nc, n_lat, CHUNK, CHUNK), bf16)],
        compiler_params=_params(("arbitrary", "arbitrary")),
    )(qkv_lat, qkv_lat, qkv_lat, qkv_ctx, qkv_ctx, be, ar, bec, arc, alog, dtb)
    return out


def _merge_kernel(x_ref, hm_ref, hd_ref, o_ref_in, z_ref, ga_ref, gb_ref, mlw_ref, dnw_ref, wa_ref, wb_ref,
                  wo_ref, nw1_ref, nw2_ref, g1_ref, sc2_ref, sh2_ref, lat_ref, h2_ref):
    hm = hm_ref[0]
    parts = []
    for h in range(ML_HEADS):
        seg = hm[:, h * ML_DV:(h + 1) * ML_DV]
        ms = jnp.mean(seg * seg, axis=-1, keepdims=True)
        parts.append(seg * lax.rsqrt(ms + EPS))
    ym = jnp.concatenate(parts, axis=-1) * mlw_ref[...] * jax.nn.sigmoid(o_ref_in[0].astype(f32))
    parts = []
    for h in range(DN_HEADS):
        seg = hd_ref[0, h]
        ms = jnp.mean(seg * seg, axis=-1, keepdims=True)
        parts.append(seg * lax.rsqrt(ms + EPS))
    z = z_ref[0].astype(f32)
    yd = jnp.concatenate(parts, axis=-1) * dnw_ref[...] * (z * jax.nn.sigmoid(z))
    ya = jnp.dot(ym.astype(bf16), wa_ref[...], preferred_element_type=f32)
    yb = jnp.dot(yd.astype(bf16), wb_ref[...], preferred_element_type=f32)
    y = jax.nn.sigmoid(ga_ref[0].astype(f32)) * ya + jax.nn.sigmoid(gb_ref[0].astype(f32)) * yb
    ymix = jnp.dot(y.astype(bf16), wo_ref[...], preferred_element_type=f32)
    ms = jnp.mean(ymix * ymix, axis=-1, keepdims=True)
    lat = x_ref[0] + g1_ref[0] * (ymix * lax.rsqrt(ms + EPS) * nw1_ref[...])
    lat_ref[0] = lat
    ms = jnp.mean(lat * lat, axis=-1, keepdims=True)
    h2 = (lat * lax.rsqrt(ms + EPS) * nw2_ref[...]) * (1.0 + sc2_ref[0]) + sh2_ref[0]
    h2_ref[0] = h2.astype(bf16)


def _merge(x, hm, hd, p_lat, mlw, dnw, wa, wb, wo, nw1, nw2, g1, sc2, sh2, *, tt):
    B, L, D = x.shape
    tok = lambda off: pl.BlockSpec((1, tt, D), lambda b, t: (b, t, off))
    full = lambda a: pl.BlockSpec(a.shape, lambda b, t: (0,) * a.ndim)
    per_b = pl.BlockSpec((1, 1, D), lambda b, t: (b, 0, 0))
    return pl.pallas_call(
        _merge_kernel,
        name="merge",
        out_shape=(jax.ShapeDtypeStruct((B, L, D), f32), jax.ShapeDtypeStruct((B, L, D), bf16)),
        grid=(B, L // tt),
        in_specs=[tok(0), tok(0),
                  pl.BlockSpec((1, DN_HEADS, tt, LANE), lambda b, t: (b, 0, t, 0)),
                  tok(2), tok(3), tok(4), tok(5),
                  full(mlw), full(dnw), full(wa), full(wb), full(wo), full(nw1), full(nw2),
                  per_b, per_b, per_b],
        out_specs=(tok(0), tok(0)),
        compiler_params=_params(("arbitrary", "arbitrary")),
    )(x, hm, hd, p_lat, p_lat, p_lat, p_lat, mlw, dnw, wa, wb, wo, nw1, nw2, g1, sc2, sh2)


def _transpose_kernel(x_ref, o_ref):
    o_ref[...] = x_ref[...].T.astype(o_ref.dtype)


def _transposed_bf16(x, *, te):
    E, D = x.shape
    return pl.pallas_call(
        _transpose_kernel,
        name="table_transpose",
        out_shape=jax.ShapeDtypeStruct((D, E), bf16),
        grid=(E // te,),
        in_specs=[pl.BlockSpec((te, D), lambda i: (i, 0))],
        out_specs=pl.BlockSpec((D, te), lambda i: (0, i)),
        compiler_params=_params(("arbitrary",)),
    )(x)


def _cand_pairs():
    return [(i, j) for i in range(PEER_TOPK) for j in range(PEER_TOPK) if (i + 1) * (j + 1) <= PEER_TOPK]


def _sorting_network(n):
    pairs = []
    p = 1
    while p < n:
        k = p
        while k >= 1:
            for j in range(k % p, n - k, 2 * k):
                for i in range(min(k, n - j - k)):
                    if (i + j) // (2 * p) == (i + j + k) // (2 * p):
                        pairs.append((i + j, i + j + k))
            k //= 2
        p *= 2
    return pairs


def _top_sorted(arrays, count):
    lists = []
    for x in arrays:
        n = x.shape[0] // 8
        xr = x.reshape(n, 8, x.shape[1])
        lists.append([xr[v] for v in range(n)])
    for i, j in _sorting_network(len(lists[0])):
        for tiles in lists:
            tiles[i], tiles[j] = jnp.maximum(tiles[i], tiles[j]), jnp.minimum(tiles[i], tiles[j])
    rows = [[] for _ in lists]
    for r in range(count):
        heads = [jnp.max(tiles[0], axis=0, keepdims=True) for tiles in lists]
        for k, m in enumerate(heads):
            rows[k].append(m)
        left = count - 1 - r
        for tiles, m in zip(lists, heads):
            hit = tiles[0] == m
            for k in range(min(left, len(tiles))):
                below = tiles[k + 1] if k + 1 < len(tiles) else NEG_INF
                tiles[k] = jnp.where(hit, below, tiles[k])
    return rows


def _peer_select(h2_ref, wq_ref, keys_ref, rank_ref, f2_ref, cnt_ref, e1_ref, qh_ref, a_ref, b_ref, cand_ref):
    n_tiles = h2_ref.shape[0] // LANE
    width = a_ref.shape[0]
    pairs = _cand_pairs()
    half = PEER_DQ // 2
    top = range(PEER_TOPK)

    def head_body(h, carry):
        qh_ref[...] = jnp.dot(h2_ref[...], wq_ref[h], preferred_element_type=f32)

        def tiles_body(g, carry2):
            lanes = range(width)
            tiles = [g * width + i for i in lanes]
            qt = [qh_ref[pl.ds(pl.multiple_of(t * LANE, LANE), LANE), :].astype(bf16) for t in tiles]
            s1 = [lax.dot_general(keys_ref[h, 0], q[:, :half], NT, preferred_element_type=f32) for q in qt]
            s2 = [lax.dot_general(keys_ref[h, 1], q[:, half:], NT, preferred_element_type=f32) for q in qt]
            tops = _top_sorted([s1[i] for i in lanes] + [s2[i] for i in lanes], PEER_TOPK)
            for i in lanes:
                for r in top:
                    a_ref[i, r:r + 1, :] = tops[i][r]
                    b_ref[i, r:r + 1, :] = tops[width + i][r]
            rank = [jnp.full(s.shape, float(PEER_TOPK), f32) for s in s2]
            for r in reversed(top):
                rank = [jnp.where(s2[i] >= b_ref[i, r:r + 1, :], float(r), rank[i]) for i in lanes]
            cand_ref[...] = jnp.full(cand_ref.shape, NEG_INF, f32)
            for c, (i1, i2) in enumerate(pairs):
                for i in lanes:
                    cand_ref[i, c:c + 1, :] = a_ref[i, i1:i1 + 1, :] + b_ref[i, i2:i2 + 1, :]
            cand = [cand_ref[i] for i in lanes]
            tau = [rows[-1] for rows in _top_sorted(cand, PEER_TOPK)]
            a0 = [a_ref[i, 0:1, :] for i in lanes]
            b0 = [b_ref[i, 0:1, :] for i in lanes]
            z = [jnp.sum(jnp.where(cand[i] >= tau[i], jnp.exp(cand[i] - (a0[i] + b0[i])), 0.0), axis=0,
                         keepdims=True) for i in lanes]
            cnt = [jnp.zeros(s.shape, f32) for s in s1]
            for r in top:
                cnt = [jnp.where(s1[i] + b_ref[i, r:r + 1, :] >= tau[i], float(r + 1), cnt[i]) for i in lanes]
            for i in lanes:
                rank_ref[h, tiles[i]] = rank[i].astype(bf16)
                f2_ref[h, tiles[i]] = (jnp.exp(s2[i] - b0[i]) * (0.5 / z[i])).astype(bf16)
                cnt_ref[h, tiles[i]] = cnt[i]
                e1_ref[h, tiles[i]] = jnp.exp(s1[i] - a0[i])
            return carry2

        return lax.fori_loop(0, n_tiles // width, tiles_body, carry)

    lax.fori_loop(0, PEER_HEADS, head_body, 0)


def _peer_kernel(h2_ref, lat_ref, wq_ref, keys_ref, u_ref, vt_ref, g2_ref, nw_ref, o_ref,
                 rank_ref, f2_ref, cnt_ref, e1_ref, qh_ref, a_ref, b_ref, cand_ref, wg_ref, yt_ref):
    j = pl.program_id(1)
    n_blk = u_ref.shape[0] // PEER_NKEYS
    n_tiles = h2_ref.shape[0] // LANE
    n_parts = 2
    per = n_blk // n_parts

    @pl.when(j == 0)
    def _():
        _peer_select(h2_ref, wq_ref, keys_ref, rank_ref, f2_ref, cnt_ref, e1_ref, qh_ref, a_ref, b_ref, cand_ref)
        yt_ref[...] = jnp.zeros_like(yt_ref)

    def gate_group(g, carry):
        t = g // n_parts
        p = g % n_parts
        ws = [None] * per
        for h in range(PEER_HEADS):
            rk = rank_ref[h, t]
            f2 = f2_ref[h, t]
            for b in range(per):
                i1 = j * n_blk + p * per + b
                cn = cnt_ref[h, t, pl.ds(i1, 1), :].astype(bf16)
                ee = e1_ref[h, t, pl.ds(i1, 1), :].astype(bf16)
                term = jnp.where(rk < cn, f2, 0) * ee
                ws[b] = term if ws[b] is None else ws[b] + term
        for b in range(per):
            r0 = pl.multiple_of((p * per + b) * PEER_NKEYS, PEER_NKEYS)
            wg_ref[t, pl.ds(r0, PEER_NKEYS), :] = ws[b]
        return carry

    lax.fori_loop(0, n_tiles * n_parts, gate_group, 0)

    at = lax.dot_general(u_ref[...], h2_ref[...], NT, preferred_element_type=f32)
    act = (at * (1.0 + lax.erf(at * (2.0 ** -0.5)))).astype(bf16)
    wg = jnp.concatenate([wg_ref[t] * act[:, t * LANE:(t + 1) * LANE] for t in range(n_tiles)], axis=1)
    yt_ref[...] += jnp.dot(vt_ref[...], wg, preferred_element_type=f32)

    @pl.when(j == pl.num_programs(1) - 1)
    def _():
        y = yt_ref[...].T
        ms = jnp.mean(y * y, axis=-1, keepdims=True)
        o_ref[...] = lat_ref[...] + g2_ref[0] * (y * lax.rsqrt(ms + EPS) * nw_ref[...])


def _peer(h2, lat, wq, keys, u, vt, g2, nw, *, tokens_per_batch, tt, ec):
    N, D = h2.shape
    E = u.shape[0]
    tiles_per_batch = tokens_per_batch // tt
    sel_width = 2 if (tt // LANE) % 2 == 0 else 1
    sel = lambda dt: pltpu.VMEM((PEER_HEADS, tt // LANE, PEER_NKEYS, LANE), dt)
    return pl.pallas_call(
        _peer_kernel,
        name="peer",
        out_shape=jax.ShapeDtypeStruct((N, D), f32),
        grid=(N // tt, E // ec),
        in_specs=[pl.BlockSpec((tt, D), lambda i, j: (i, 0)),
                  pl.BlockSpec((tt, D), lambda i, j: (i, 0)),
                  pl.BlockSpec(wq.shape, lambda i, j: (0, 0, 0)),
                  pl.BlockSpec(keys.shape, lambda i, j: (0, 0, 0, 0)),
                  pl.BlockSpec((ec, D), lambda i, j: (j, 0)),
                  pl.BlockSpec((D, ec), lambda i, j: (0, j)),
                  pl.BlockSpec((1, 1, D), lambda i, j: (i // tiles_per_batch, 0, 0)),
                  pl.BlockSpec((1, D), lambda i, j: (0, 0))],
        out_specs=pl.BlockSpec((tt, D), lambda i, j: (i, 0)),
        scratch_shapes=[sel(bf16), sel(bf16), sel(f32), sel(f32),
                        pltpu.VMEM((tt, PEER_DQ), f32),
                        pltpu.VMEM((sel_width, PEER_TOPK, LANE), f32),
                        pltpu.VMEM((sel_width, PEER_TOPK, LANE), f32),
                        pltpu.VMEM((sel_width, CHUNK, LANE), f32),
                        pltpu.VMEM((tt // LANE, ec, LANE), bf16),
                        pltpu.VMEM((D, tt), f32)],
        compiler_params=_params(("arbitrary", "arbitrary")),
    )(h2, lat, wq, keys, u, vt, g2, nw)


def _dir_rows(t, n_heads, width=CHUNK):
    B, L, _ = t.shape
    return t.reshape(B, L, N_DIR, n_heads).transpose(0, 2, 3, 1).reshape(B, N_DIR, n_heads, L // width, width)


def _to_colmajor(t, rows):
    B, _, C = t.shape
    return t.reshape(B, rows, GRID_W, C).transpose(0, 2, 1, 3).reshape(B, GRID_W * rows, C)


def _pick(n, prefs):
    for p in prefs:
        if n % p == 0:
            return p
    raise ValueError(f"no tile in {prefs} divides {n}")


def kernel(x, c, ctx, c_ctx, w_mod, b_mod, norm_w, w_in, ml_gate_bias, dn_a_log, dn_dt_bias, dn_conv_w,
           ml_norm_w, dn_norm_w, w_branch_a, w_branch_b, w_out, peer_wq, peer_keys, peer_u, peer_v):
    B, L, D = x.shape
    Lc = ctx.shape[1]
    assert w_mod.shape[0] == 1, "single-layer block"
    assert L % GRID_W == 0 and L % CHUNK == 0 and Lc % CHUNK == 0
    rows = L // GRID_W
    assert rows % 8 == 0 and CHUNK % rows == 0
    assert D == ML_HEADS * ML_DV == DN_HEADS * DN_DV

    mod = _modulation(jnp.concatenate([c, c_ctx[None, :]], axis=0), w_mod[0], b_mod[0])
    mod = mod.reshape(B + 1, N_MOD, 1, D)
    sh1, sc1, g1, sh2, sc2, g2 = (mod[:B, i] for i in range(N_MOD))
    sh1c, sc1c = mod[B:, 0], mod[B:, 1]

    w = w_in[0]
    sizes = (512, 512, 1024, 1024, 8, 8, 1024, 1024, 1024, 1024, 16, 16, 1024, 1024)
    offs = [0]
    for s in sizes:
        offs.append(offs[-1] + s)
    col = lambda i: w[:, offs[i]:offs[i + 1]]
    w_main = jnp.concatenate([col(0), col(1), col(2), col(3), col(9), col(12), col(13)], axis=1).astype(bf16)
    w_dn = jnp.concatenate([col(6), col(7), col(8)], axis=1).astype(bf16)
    w_gate = jnp.concatenate([col(4), col(5), col(10), col(11)], axis=1)
    n_gate = w_gate.shape[1]
    w_gate = jnp.pad(w_gate, ((0, 0), (0, LANE - n_gate))).astype(bf16)

    nw = norm_w[0]
    tt_lat = _pick(L, (1024, 512, 256, 128, 64))
    tt_ctx = _pick(Lc, (256, 128, 64))
    proj = lambda xs, sc_, sh_, tt: (
        _project(xs, nw[0:1], sc_, sh_, w_main, tn=1024, tt=tt, head_major=False, out_dtype=bf16),
        _project(xs, nw[0:1], sc_, sh_, w_dn, tn=1024, tt=tt, head_major=True, out_dtype=f32),
        _project(xs, nw[0:1], sc_, sh_, w_gate, tn=LANE, tt=tt, head_major=False, out_dtype=f32))
    p_lat, dn_lat, g_lat = proj(x, sc1, sh1, tt_lat)
    p_ctx, dn_ctx, g_ctx = proj(ctx, sc1c, sh1c, tt_ctx)

    nh = N_DIR * ML_HEADS
    nd = N_DIR * DN_HEADS
    gi, gf = _dir_rows(g_lat[..., 0:nh], ML_HEADS), _dir_rows(g_lat[..., nh:2 * nh], ML_HEADS)
    gic, gfc = _dir_rows(g_ctx[..., 0:nh], ML_HEADS), _dir_rows(g_ctx[..., nh:2 * nh], ML_HEADS)
    dn_width = CHUNK * math.gcd(math.gcd(L // CHUNK, Lc // CHUNK), 4)
    be = _dir_rows(_to_colmajor(g_lat[..., 2 * nh:2 * nh + nd], rows), DN_HEADS, dn_width)
    ar = _dir_rows(_to_colmajor(g_lat[..., 2 * nh + nd:2 * nh + 2 * nd], rows), DN_HEADS, dn_width)
    bec = _dir_rows(g_ctx[..., 2 * nh:2 * nh + nd], DN_HEADS, dn_width)
    arc = _dir_rows(g_ctx[..., 2 * nh + nd:2 * nh + 2 * nd], DN_HEADS, dn_width)
    bias = jnp.broadcast_to(ml_gate_bias[0].astype(f32)[:, :, :, None, None], (2, N_DIR, ML_HEADS, 1, CHUNK))
    alog = jnp.broadcast_to(dn_a_log[0].astype(f32)[:, :, None, None], (N_DIR, DN_HEADS, 1, dn_width))
    dtb = jnp.broadcast_to(dn_dt_bias[0].astype(f32)[:, :, None, None], (N_DIR, DN_HEADS, 1, dn_width))

    hm = _mlstm(p_lat, p_ctx, gi, gf, gic, gfc, bias)

    taps = dn_conv_w[0].astype(f32).reshape(DN_CONV, 3 * DN_HEADS, LANE).transpose(1, 0, 2)
    taps = jnp.pad(taps, ((0, 0), (0, 8 - DN_CONV), (0, 0)))
    qkv_lat = _dn_prep(dn_lat, taps, seq_len=rows, n_cols=GRID_W, heads_per_step=2)
    qkv_ctx = _dn_prep(dn_ctx, taps, seq_len=Lc, n_cols=1, heads_per_step=DN_HEADS)
    hd = _gdn(qkv_lat, qkv_ctx, be, ar, bec, arc, alog, dtb, rows=rows)

    lat1, h2 = _merge(x, hm, hd, p_lat, ml_norm_w[0].reshape(1, D).astype(f32),
                      jnp.tile(dn_norm_w[0].astype(f32), DN_HEADS).reshape(1, D),
                      w_branch_a[0].astype(bf16), w_branch_b[0].astype(bf16), w_out[0].astype(bf16),
                      nw[1:2], nw[2:3], g1, sc2, sh2, tt=_pick(L, (256, 128, 64)))

    wq = peer_wq[0].astype(bf16).reshape(D, PEER_HEADS, PEER_DQ).transpose(1, 0, 2)
    out = _peer(h2.reshape(B * L, D), lat1.reshape(B * L, D), wq, peer_keys[0].astype(bf16),
                peer_u[0].astype(bf16), _transposed_bf16(peer_v[0], te=1024), g2, nw[3:4],
                tokens_per_batch=L, tt=_pick(L, (512, 256)), ec=2048)
    return out.reshape(B, L, D)
```
